```python
import jax, jax.numpy as jnp
from jax import lax
import numpy as np

D_MODEL = 1024
BATCH = 32
SEQ = 2048
DEPTH = 1

SB_HEADS = 8
SB_HEAD_DIM = 64
SB_WIDTH = SB_HEADS * SB_HEAD_DIM
SB_BLOCK = 128

GLA_HEADS = 4
GLA_DK = 64
GLA_DV = 128
GLA_KW = GLA_HEADS * GLA_DK
GLA_VW = GLA_HEADS * GLA_DV
GLA_RANK = 16
GLA_TAU = 16.0
GLA_CHUNK = 64

D_FF = 2816
CONV_WIDTH = 3

EPS = 1e-6

IN_SIZES = (SB_WIDTH, SB_WIDTH, SB_WIDTH,
            GLA_KW, GLA_KW, GLA_VW, GLA_VW,
            GLA_RANK,
            D_MODEL, D_MODEL)
IN_TOTAL = sum(IN_SIZES)

kernel_name = "hybrid_stickbreak_gla_convffn"


def rms_norm(x, g):
    xf = x.astype(jnp.float32)
    xf = xf * lax.rsqrt(jnp.mean(xf * xf, axis=-1, keepdims=True) + EPS)
    return xf.astype(x.dtype) * g


def to_heads(t, n_heads):
    b, s, _ = t.shape
    return t.reshape(b, s, n_heads, -1).transpose(0, 2, 1, 3)


def from_heads(t):
    b, h, s, d = t.shape
    return t.transpose(0, 2, 1, 3).reshape(b, s, h * d)


def stick_breaking_attention(q, k, v):
    s_len, dh = q.shape[2], q.shape[3]
    scale = dh ** -0.5
    outs = []
    for i0 in range(0, s_len, SB_BLOCK):
        end = i0 + SB_BLOCK
        qb = q[:, :, i0:end]
        kb = k[:, :, :end]
        vb = v[:, :, :end]
        z = jnp.einsum('bhqd,bhkd->bhqk', qb, kb).astype(jnp.float32) * scale
        t_idx = i0 + jnp.arange(SB_BLOCK)[:, None]
        s_idx = jnp.arange(end)[None, :]
        strict = s_idx < t_idx
        log_keep = jnp.where(strict, jax.nn.log_sigmoid(-z), 0.0)
        tail = lax.cumsum(log_keep, axis=3, reverse=True) - log_keep
        log_w = jax.nn.log_sigmoid(z) + tail
        w = jnp.where(strict, jnp.exp(log_w), 0.0)
        outs.append(jnp.einsum('bhqk,bhkd->bhqd', w.astype(v.dtype), vb))
    return jnp.concatenate(outs, axis=2)


def gla_chunked(q, k, v, log_a):
    b, h, s_len, dk = q.shape
    dv = v.shape[-1]
    c = GLA_CHUNK
    n = s_len // c
    qf = q.astype(jnp.float32).reshape(b, h, n, c, dk)
    kf = k.astype(jnp.float32).reshape(b, h, n, c, dk)
    vf = v.astype(jnp.float32).reshape(b, h, n, c, dv)
    cum = jnp.cumsum(log_a.reshape(b, h, n, c, dk), axis=3)
    cum_last = cum[:, :, :, -1:]
    q_dec = qf * jnp.exp(cum)
    k_inv = kf * jnp.exp(-cum)
    k_to_end = kf * jnp.exp(cum_last - cum)
    att = jnp.einsum('bhnck,bhnsk->bhncs', q_dec, k_inv)
    causal = jnp.tril(jnp.ones((c, c), dtype=bool))
    att = jnp.where(causal, att, 0.0)
    o_intra = jnp.einsum('bhncs,bhnsv->bhncv', att, vf)
    d_state = jnp.einsum('bhnck,bhncv->bhnkv', k_to_end, vf)
    chunk_decay = jnp.exp(cum_last[:, :, :, 0])

    def step(state, inp):
        dec, ds = inp
        return dec[..., None] * state + ds, state

    init = jnp.zeros((b, h, dk, dv), jnp.float32)
    _, states_before = lax.scan(step, init,
                                (jnp.moveaxis(chunk_decay, 2, 0), jnp.moveaxis(d_state, 2, 0)))
    states_before = jnp.moveaxis(states_before, 0, 2)
    o_inter = jnp.einsum('bhnck,bhnkv->bhncv', q_dec, states_before)
    return (o_intra + o_inter).reshape(b, h, s_len, dv)


def mixing_sublayer(xn, w_in, b_gate, w_alpha_up, b_alpha, gla_norm_g,
                    w_branch_sb, w_branch_gla, w_out):
    proj = xn @ w_in
    offsets = np.cumsum(IN_SIZES)[:-1].tolist()
    (sb_q, sb_k, sb_v, g_q, g_k, g_v, g_r, g_a,
     gate_sb, gate_gla) = jnp.split(proj, offsets, axis=-1)

    o_sb = stick_breaking_attention(to_heads(sb_q, SB_HEADS),
                                    to_heads(sb_k, SB_HEADS),
                                    to_heads(sb_v, SB_HEADS))
    o_sb = from_heads(o_sb)

    a_pre = (g_a @ w_alpha_up + b_alpha).astype(jnp.float32)
    log_a = jax.nn.log_sigmoid(a_pre) / GLA_TAU
    o_gla = gla_chunked(to_heads(g_q, GLA_HEADS) * (GLA_DK ** -0.5),
                        to_heads(g_k, GLA_HEADS),
                        to_heads(g_v, GLA_HEADS),
                        to_heads(log_a, GLA_HEADS))
    o_gla = o_gla * lax.rsqrt(jnp.mean(o_gla * o_gla, axis=-1, keepdims=True) + EPS)
    o_gla = from_heads(o_gla).astype(xn.dtype) * gla_norm_g * jax.nn.silu(g_r)

    y = (jax.nn.sigmoid(gate_sb + b_gate[0]) * (o_sb @ w_branch_sb)
         + jax.nn.sigmoid(gate_gla + b_gate[1]) * (o_gla @ w_branch_gla))
    return y @ w_out


def causal_depthwise_conv(u, w, bias):
    k_w = w.shape[0]
    s_len = u.shape[1]
    up = jnp.pad(u, ((0, 0), (k_w - 1, 0), (0, 0)))
    out = bias
    for i in range(k_w):
        out = out + up[:, i:i + s_len] * w[i]
    return out


def conv_ffn(hn, w_ffn_in, conv_w, conv_b, w_ffn_out):
    u = hn @ w_ffn_in
    a, g = jnp.split(u, 2, axis=-1)
    a = causal_depthwise_conv(a, conv_w, conv_b)
    return (jax.nn.gelu(a) * g) @ w_ffn_out


def setup_inputs(seed: int = 0) -> dict:
    key = jax.random.key(seed)
    ks = jax.random.split(key, 16)
    f32 = jnp.float32

    def nrm(k, shape, scale):
        return jax.random.normal(k, shape, f32) * scale

    return {
        "x": jax.random.normal(ks[0], (BATCH, SEQ, D_MODEL), f32),
        "norm_mix_g": 1.0 + nrm(ks[1], (DEPTH, D_MODEL), 0.02),
        "w_in": nrm(ks[2], (DEPTH, D_MODEL, IN_TOTAL), D_MODEL ** -0.5),
        "b_gate": nrm(ks[3], (DEPTH, 2, D_MODEL), 0.02),
        "w_alpha_up": nrm(ks[4], (DEPTH, GLA_RANK, GLA_KW), GLA_RANK ** -0.5),
        "b_alpha": nrm(ks[5], (DEPTH, GLA_KW), 0.1),
        "gla_norm_g": 1.0 + nrm(ks[6], (DEPTH, GLA_VW), 0.02),
        "w_branch_sb": nrm(ks[7], (DEPTH, SB_WIDTH, D_MODEL), SB_WIDTH ** -0.5),
        "w_branch_gla": nrm(ks[8], (DEPTH, GLA_VW, D_MODEL), GLA_VW ** -0.5),
        "w_out": nrm(ks[9], (DEPTH, D_MODEL, D_MODEL), D_MODEL ** -0.5),
        "norm_ffn_g": 1.0 + nrm(ks[10], (DEPTH, D_MODEL), 0.02),
        "w_ffn_in": nrm(ks[11], (DEPTH, D_MODEL, 2 * D_FF), D_MODEL ** -0.5),
        "conv_w": nrm(ks[12], (DEPTH, CONV_WIDTH, D_FF), CONV_WIDTH ** -0.5),
        "conv_b": nrm(ks[13], (DEPTH, D_FF), 0.01),
        "w_ffn_out": nrm(ks[14], (DEPTH, D_FF, D_MODEL), D_FF ** -0.5),
        "norm_final_g": 1.0 + nrm(ks[15], (D_MODEL,), 0.02),
    }


def reference(x, norm_mix_g, w_in, b_gate, w_alpha_up, b_alpha, gla_norm_g,
              w_branch_sb, w_branch_gla, w_out, norm_ffn_g, w_ffn_in, conv_w,
              conv_b, w_ffn_out, norm_final_g):
    h = x
    for i in range(DEPTH):
        xn = rms_norm(h, norm_mix_g[i])
        h = h + mixing_sublayer(xn, w_in[i], b_gate[i], w_alpha_up[i], b_alpha[i],
                                gla_norm_g[i], w_branch_sb[i], w_branch_gla[i], w_out[i])
        hn = rms_norm(h, norm_ffn_g[i])
        h = h + conv_ffn(hn, w_ffn_in[i], conv_w[i], conv_b[i], w_ffn_out[i])
    return rms_norm(h, norm_final_g)
```

```python
import functools

import jax
import jax.numpy as jnp
from jax import lax
from jax.experimental import pallas as pl
from jax.experimental.pallas import tpu as pltpu

F32 = jnp.float32
BF16 = jnp.bfloat16

EPS = 1e-6
SB_HEADS = 8
SB_DH = 64
GLA_HEADS = 4
GLA_DK = 64
GLA_DV = 128
GLA_RANK = 16
GLA_TAU = 16.0
GLA_CHUNK = 64
CONV_WIDTH = 3

LANES = 128
SUBLANES = 8
VMEM_LIMIT = 56 * 1024 * 1024

TM = 512
SB_TQ = 256
SB_TK = 256
GLA_BLK = 512
FF_CK = 256


def _nt_dot(a, b):
    return lax.dot_general(a, b, (((1,), (1,)), ((), ())), preferred_element_type=F32)


def _dot(a, b):
    return jnp.dot(a, b, preferred_element_type=F32)


def _log_sigmoid(z):
    return jnp.minimum(z, 0.0) - jnp.log(1.0 + jnp.exp(-jnp.abs(z)))


def _sigmoid(z):
    return 1.0 / (1.0 + jnp.exp(-z))


def _const_spec(shape):
    return pl.BlockSpec(shape, lambda *_: (0,) * len(shape))


def _in_proj_kernel(x_ref, g_ref, wqkv_ref, wa_ref, wup_ref, balpha_ref, wgate_ref, bgate_ref,
                    tri_ref, sbq_ref, sbk_ref, sbv_ref, gq_ref, gk_ref, gv_ref, gr_ref, cum_ref,
                    sgs_ref, sgg_ref):
    x = x_ref[...]
    ms = jnp.mean(x * x, axis=-1, keepdims=True)
    xn = (x * lax.rsqrt(ms + EPS) * g_ref[...]).astype(BF16)

    def proj(lo, hi):
        return _dot(xn, wqkv_ref[:, lo:hi])

    sb_w = SB_HEADS * SB_DH
    kw = GLA_HEADS * GLA_DK
    vw = GLA_HEADS * GLA_DV
    o = 0
    sbq_ref[...] = (proj(o, o + sb_w) * (SB_DH ** -0.5)).astype(BF16); o += sb_w
    sbk_ref[...] = proj(o, o + sb_w).astype(BF16); o += sb_w
    sbv_ref[...] = proj(o, o + sb_w).astype(BF16); o += sb_w
    gq_ref[...] = (proj(o, o + kw) * (GLA_DK ** -0.5)).astype(BF16); o += kw
    gk_ref[...] = proj(o, o + kw).astype(BF16); o += kw
    gv_ref[...] = proj(o, o + vw).astype(BF16); o += vw
    r = proj(o, o + vw)
    gr_ref[...] = (r * _sigmoid(r)).astype(BF16)

    code = _dot(xn, wa_ref[...])
    a_pre = _dot(code.astype(BF16), wup_ref[...]) + balpha_ref[...]
    log_a = _log_sigmoid(a_pre) * (1.0 / GLA_TAU)
    la_hi = log_a.astype(BF16)
    la_lo = (log_a - la_hi.astype(F32)).astype(BF16)
    tri = tri_ref[...]
    cum_ref[...] = _dot(tri, la_hi) + _dot(tri, la_lo)

    d = sgs_ref.shape[-1]
    sgs_ref[...] = _sigmoid(_dot(xn, wgate_ref[:, :d]) + bgate_ref[0:1, :]).astype(BF16)
    sgg_ref[...] = _sigmoid(_dot(xn, wgate_ref[:, d:]) + bgate_ref[1:2, :]).astype(BF16)


def _in_proj(x2, norm_g, w_in, b_gate, w_alpha_up, b_alpha):
    t, d = x2.shape
    sb_w = SB_HEADS * SB_DH
    kw = GLA_HEADS * GLA_DK
    vw = GLA_HEADS * GLA_DV
    n_qkv = 3 * sb_w + 2 * kw + 2 * vw
    wqkv = w_in[:, :n_qkv].astype(BF16)
    wa = jnp.pad(w_in[:, n_qkv:n_qkv + GLA_RANK], ((0, 0), (0, LANES - GLA_RANK))).astype(BF16)
    wgate = w_in[:, n_qkv + GLA_RANK:].astype(BF16)
    wup = jnp.pad(w_alpha_up, ((0, LANES - GLA_RANK), (0, 0))).astype(BF16)
    row = lax.broadcasted_iota(jnp.int32, (TM, TM), 0)
    col = lax.broadcasted_iota(jnp.int32, (TM, TM), 1)
    tri = ((row // GLA_CHUNK == col // GLA_CHUNK) & (col <= row)).astype(BF16)

    tok = lambda w: pl.BlockSpec((TM, w), lambda i: (i, 0))
    out_shapes = [jax.ShapeDtypeStruct((t, w), BF16) for w in (sb_w, sb_w, sb_w, kw, kw, vw, vw)]
    out_shapes += [jax.ShapeDtypeStruct((t, kw), F32),
                   jax.ShapeDtypeStruct((t, d), BF16), jax.ShapeDtypeStruct((t, d), BF16)]
    out_specs = [tok(s.shape[1]) for s in out_shapes]
    return pl.pallas_call(
        _in_proj_kernel,
        grid=(t // TM,),
        in_specs=[tok(d), _const_spec((1, d)), _const_spec(wqkv.shape), _const_spec(wa.shape),
                  _const_spec(wup.shape), _const_spec((1, kw)), _const_spec(wgate.shape),
                  _const_spec((2, d)), _const_spec((TM, TM))],
        out_specs=out_specs,
        out_shape=out_shapes,
        compiler_params=pltpu.CompilerParams(dimension_semantics=("arbitrary",),
                                             vmem_limit_bytes=VMEM_LIMIT),
        name="in_proj",
    )(x2, norm_g.reshape(1, d), wqkv, wa, wup, b_alpha.reshape(1, kw), wgate, b_gate, tri)


def _sb_kernel(q_ref, k_ref, v_ref, tri_ref, o_ref):
    qi = pl.program_id(2)
    q = q_ref[0]
    lane = lax.broadcasted_iota(jnp.int32, (1, LANES), 1)
    row = lax.broadcasted_iota(jnp.int32, (SB_TQ, SB_TK), 0)
    col = lax.broadcasted_iota(jnp.int32, (SB_TQ, SB_TK), 1)
    strict = col < row
    tri = tri_ref[...]
    zero = jnp.zeros((), BF16)

    def tile(qh, hmask, kj, carry, acc, diag):
        ks = pl.multiple_of(kj * SB_TK, SB_TK)
        kt = k_ref[0, pl.ds(ks, SB_TK), :]
        vt = jnp.where(hmask, v_ref[0, pl.ds(ks, SB_TK), :], zero)
        z = _nt_dot(qh, kt)
        log_beta = _log_sigmoid(z)
        log_keep = log_beta - z
        if diag:
            log_keep = jnp.where(strict, log_keep, 0.0)
        tail = _dot(log_keep.astype(BF16), tri) + carry
        w = jnp.exp(log_beta + tail)
        if diag:
            w = jnp.where(strict, w, 0.0)
        acc = acc + _dot(w.astype(BF16), vt)
        carry = carry + jnp.sum(log_keep, axis=1, keepdims=True)
        return carry, acc

    acc = jnp.zeros((SB_TQ, LANES), F32)
    for hh in range(2):
        hmask = (lane >= hh * SB_DH) & (lane < (hh + 1) * SB_DH)
        qh = jnp.where(hmask, q, zero)
        carry = jnp.zeros((SB_TQ, 1), F32)
        carry, acc = tile(qh, hmask, qi, carry, acc, True)

        def body(i, c, qh=qh, hmask=hmask):
            return tile(qh, hmask, qi - 1 - i, c[0], c[1], False)

        carry, acc = lax.fori_loop(0, qi, body, (carry, acc))
    o_ref[0] = acc.astype(BF16)


def _sb_attn(q, k, v):
    b, s, w = q.shape
    row = lax.broadcasted_iota(jnp.int32, (SB_TK, SB_TK), 0)
    col = lax.broadcasted_iota(jnp.int32, (SB_TK, SB_TK), 1)
    tri = (row > col).astype(BF16)
    qspec = pl.BlockSpec((1, SB_TQ, LANES), lambda bi, p, qi: (bi, qi, p))
    kvspec = pl.BlockSpec((1, s, LANES), lambda bi, p, qi: (bi, 0, p))
    return pl.pallas_call(
        _sb_kernel,
        grid=(b, w // LANES, s // SB_TQ),
        in_specs=[qspec, kvspec, kvspec, _const_spec((SB_TK, SB_TK))],
        out_specs=qspec,
        out_shape=jax.ShapeDtypeStruct((b, s, w), BF16),
        compiler_params=pltpu.CompilerParams(
            dimension_semantics=("arbitrary", "arbitrary", "arbitrary"),
            vmem_limit_bytes=VMEM_LIMIT),
        name="sb_attn",
    )(q, k, v, tri)


def _gla_kernel(q_ref, k_ref, v_ref, cum_ref, r_ref, g_ref, o_ref, st_ref):
    @pl.when(pl.program_id(1) == 0)
    def _():
        st_ref[...] = jnp.zeros_like(st_ref)

    c = GLA_CHUNK
    lane = lax.broadcasted_iota(jnp.int32, (1, LANES), 1)
    row = lax.broadcasted_iota(jnp.int32, (c, c), 0)
    col = lax.broadcasted_iota(jnp.int32, (c, c), 1)
    causal = col <= row
    for n in range(GLA_BLK // c):
        rows = slice(n * c, (n + 1) * c)
        for p in range(GLA_HEADS // 2):
            lanes = slice(p * LANES, (p + 1) * LANES)
            cum = cum_ref[0, rows, lanes]
            cum_last = cum[c - 1:c, :]
            qf = q_ref[0, rows, lanes].astype(F32)
            kf = k_ref[0, rows, lanes].astype(F32)
            q_dec = qf * jnp.exp(cum)
            k_inv = (kf * jnp.exp(-cum)).astype(BF16)
            k_end = kf * jnp.exp(cum_last - cum)
            decay = jnp.exp(cum_last)
            for hh in range(2):
                h = 2 * p + hh
                hmask = (lane >= hh * GLA_DK) & (lane < (hh + 1) * GLA_DK)
                vlanes = slice(h * GLA_DV, (h + 1) * GLA_DV)
                vb = v_ref[0, rows, vlanes]
                qd = jnp.where(hmask, q_dec, 0.0).astype(BF16)
                ke = jnp.where(hmask, k_end, 0.0).astype(BF16)
                att = jnp.where(causal, _nt_dot(qd, k_inv), 0.0)
                st = st_ref[h]
                o = _dot(att.astype(BF16), vb) + _nt_dot(qd, st.astype(BF16))
                vt = vb.astype(F32).T.astype(BF16)
                st_ref[h] = st * decay + _dot(vt, ke)
                o = o * lax.rsqrt(jnp.mean(o * o, axis=-1, keepdims=True) + EPS)
                o = o * g_ref[0:1, vlanes] * r_ref[0, rows, vlanes].astype(F32)
                o_ref[0, rows, vlanes] = o.astype(BF16)


def _gla(gq, gk, gv, cum, r_act, norm_g):
    b, s, kw = gq.shape
    vw = gv.shape[-1]
    spec = lambda w: pl.BlockSpec((1, GLA_BLK, w), lambda bi, j: (bi, j, 0))
    return pl.pallas_call(
        _gla_kernel,
        grid=(b, s // GLA_BLK),
        in_specs=[spec(kw), spec(kw), spec(vw), spec(kw), spec(vw), _const_spec((1, vw))],
        out_specs=spec(vw),
        out_shape=jax.ShapeDtypeStruct((b, s, vw), BF16),
        scratch_shapes=[pltpu.VMEM((GLA_HEADS, GLA_DV, LANES), F32)],
        compiler_params=pltpu.CompilerParams(dimension_semantics=("arbitrary", "arbitrary"),
                                             vmem_limit_bytes=VMEM_LIMIT),
        name="gla",
    )(gq, gk, gv, cum, r_act, norm_g.reshape(1, vw))


def _mix_out_kernel(x_ref, osb_ref, ogla_ref, sgs_ref, sgg_ref, wsb_ref, wgla_ref, wout_ref, h_ref):
    y = (sgs_ref[...].astype(F32) * _dot(osb_ref[...], wsb_ref[...])
         + sgg_ref[...].astype(F32) * _dot(ogla_ref[...], wgla_ref[...]))
    h_ref[...] = x_ref[...] + _dot(y.astype(BF16), wout_ref[...])


def _mix_out(x2, o_sb, o_gla, sg_sb, sg_gla, w_branch_sb, w_branch_gla, w_out):
    t, d = x2.shape
    tok = lambda w: pl.BlockSpec((TM, w), lambda i: (i, 0))
    wsb, wgla, wout = (w.astype(BF16) for w in (w_branch_sb, w_branch_gla, w_out))
    return pl.pallas_call(
        _mix_out_kernel,
        grid=(t // TM,),
        in_specs=[tok(d), tok(o_sb.shape[1]), tok(o_gla.shape[1]), tok(d), tok(d),
                  _const_spec(wsb.shape), _const_spec(wgla.shape), _const_spec(wout.shape)],
        out_specs=tok(d),
        out_shape=jax.ShapeDtypeStruct((t, d), F32),
        compiler_params=pltpu.CompilerParams(dimension_semantics=("arbitrary",),
                                             vmem_limit_bytes=VMEM_LIMIT),
        name="mix_out",
    )(x2, o_sb, o_gla, sg_sb, sg_gla, wsb, wgla, wout)


def _gelu_tanh(x):
    return 0.5 * x * (1.0 + jnp.tanh(0.7978845608028654 * (x + 0.044715 * (x * x * x))))


def _conv_ffn_kernel(h_ref, g_ref, win_ref, cw_ref, cb_ref, wout_ref, gf_ref, o_ref,
                     acc_ref, prev_ref, *, tiles_per_seq, d_ff, final_norm):
    @pl.when(pl.program_id(0) % tiles_per_seq == 0)
    def _():
        prev_ref[...] = jnp.zeros_like(prev_ref)

    h = h_ref[...]
    ms = jnp.mean(h * h, axis=-1, keepdims=True)
    hn = (h * lax.rsqrt(ms + EPS) * g_ref[...]).astype(BF16)
    tm = h.shape[0]
    for ci in range(d_ff // FF_CK):
        cols = slice(ci * FF_CK, (ci + 1) * FF_CK)
        a = _dot(hn, win_ref[:, cols])
        gate = _dot(hn, win_ref[:, d_ff + ci * FF_CK:d_ff + (ci + 1) * FF_CK])
        ext = jnp.concatenate([prev_ref[:, cols], a], axis=0)
        prev_ref[:, cols] = a[tm - SUBLANES:, :]
        a1 = pltpu.roll(ext, 1, 0)[SUBLANES:, :]
        a2 = pltpu.roll(ext, 2, 0)[SUBLANES:, :]
        conv = (cb_ref[0:1, cols] + a2 * cw_ref[0:1, cols] + a1 * cw_ref[1:2, cols]
                + a * cw_ref[2:3, cols])
        act = (_gelu_tanh(conv) * gate).astype(BF16)
        part = _dot(act, wout_ref[cols, :])
        if ci == 0:
            acc_ref[...] = part
        else:
            acc_ref[...] += part
    h2 = h + acc_ref[...]
    if final_norm:
        ms2 = jnp.mean(h2 * h2, axis=-1, keepdims=True)
        h2 = h2 * lax.rsqrt(ms2 + EPS) * gf_ref[...]
    o_ref[...] = h2


def _conv_ffn(h2d, seq_len, norm_g, w_ffn_in, conv_w, conv_b, w_ffn_out, norm_final_g, final_norm):
    t, d = h2d.shape
    d_ff = w_ffn_out.shape[0]
    tok = pl.BlockSpec((TM, d), lambda i: (i, 0))
    win = w_ffn_in.astype(BF16)
    wout = w_ffn_out.astype(BF16)
    kern = functools.partial(_conv_ffn_kernel, tiles_per_seq=seq_len // TM, d_ff=d_ff,
                             final_norm=final_norm)
    return pl.pallas_call(
        kern,
        grid=(t // TM,),
        in_specs=[tok, _const_spec((1, d)), _const_spec(win.shape), _const_spec(conv_w.shape),
                  _const_spec((1, d_ff)), _const_spec(wout.shape), _const_spec((1, d))],
        out_specs=tok,
        out_shape=jax.ShapeDtypeStruct((t, d), F32),
        scratch_shapes=[pltpu.VMEM((TM, d), F32), pltpu.VMEM((SUBLANES, d_ff), F32)],
        compiler_params=pltpu.CompilerParams(dimension_semantics=("arbitrary",),
                                             vmem_limit_bytes=VMEM_LIMIT),
        name="conv_ffn",
    )(h2d, norm_g.reshape(1, d), win, conv_w, conv_b.reshape(1, d_ff), wout,
      norm_final_g.reshape(1, d))


def kernel(x, norm_mix_g, w_in, b_gate, w_alpha_up, b_alpha, gla_norm_g, w_branch_sb, w_branch_gla,
           w_out, norm_ffn_g, w_ffn_in, conv_w, conv_b, w_ffn_out, norm_final_g):
    b, s, d = x.shape
    depth = w_in.shape[0]
    assert s % TM == 0 and s % SB_TQ == 0 and s % GLA_BLK == 0 and (b * s) % TM == 0
    assert w_ffn_out.shape[1] % FF_CK == 0 and conv_w.shape[1] == CONV_WIDTH
    h = x.reshape(b * s, d)
    for i in range(depth):
        (sbq, sbk, sbv, gq, gk, gv, r_act, cum, sg_sb, sg_gla) = _in_proj(
            h, norm_mix_g[i], w_in[i], b_gate[i], w_alpha_up[i], b_alpha[i])
        seq = lambda a: a.reshape(b, s, a.shape[-1])
        o_sb = _sb_attn(seq(sbq), seq(sbk), seq(sbv)).reshape(b * s, -1)
        o_gla = _gla(seq(gq), seq(gk), seq(gv), seq(cum), seq(r_act), gla_norm_g[i])
        h = _mix_out(h, o_sb, o_gla.reshape(b * s, -1), sg_sb, sg_gla,
                     w_branch_sb[i], w_branch_gla[i], w_out[i])
        h = _conv_ffn(h, s, norm_ffn_g[i], w_ffn_in[i], conv_w[i], conv_b[i], w_ffn_out[i],
                      norm_final_g, final_norm=(i == depth - 1))
    return h.reshape(b, s, d)
```

```python
import functools

import jax
import jax.numpy as jnp
from jax import lax
from jax.experimental import pallas as pl
from jax.experimental.pallas import tpu as pltpu

F32 = jnp.float32
BF16 = jnp.bfloat16

EPS = 1e-6
LOG2_E = 1.4426950408889634
SB_HEADS = 8
SB_DH = 64
GLA_HEADS = 4
GLA_DK = 64
GLA_DV = 128
GLA_RANK = 16
GLA_TAU = 16.0
GLA_CHUNK = 64
CONV_WIDTH = 3

LANES = 128
SUBLANES = 8
VMEM_LIMIT = 56 * 1024 * 1024

TM = 512
SB_TQ = 256
SB_TK = 256
GLA_BLK = 512
FF_CK = 256


def _nt_dot(a, b):
    return lax.dot_general(a, b, (((1,), (1,)), ((), ())), preferred_element_type=F32)


def _dot(a, b):
    return jnp.dot(a, b, preferred_element_type=F32)


def _log_sigmoid(z):
    return jnp.minimum(z, 0.0) - jnp.log(1.0 + jnp.exp(-jnp.abs(z)))


def _sigmoid(z):
    return 1.0 / (1.0 + jnp.exp(-z))


def _const_spec(shape):
    return pl.BlockSpec(shape, lambda *_: (0,) * len(shape))


def _in_proj_kernel(x_ref, g_ref, wqkv_ref, wa_ref, wup_ref, balpha_ref, wgate_ref, bgate_ref,
                    tri_ref, sbq_ref, sbk_ref, sbv_ref, gq_ref, gk_ref, gv_ref, gr_ref, cum_ref,
                    sgs_ref, sgg_ref):
    x = x_ref[...]
    ms = jnp.mean(x * x, axis=-1, keepdims=True)
    xn = (x * lax.rsqrt(ms + EPS) * g_ref[...]).astype(BF16)

    def proj(lo, hi):
        return _dot(xn, wqkv_ref[:, lo:hi])

    sb_w = SB_HEADS * SB_DH
    kw = GLA_HEADS * GLA_DK
    vw = GLA_HEADS * GLA_DV
    o = 0
    sbq_ref[...] = (proj(o, o + sb_w) * (SB_DH ** -0.5 * LOG2_E)).astype(BF16); o += sb_w
    sbk_ref[...] = proj(o, o + sb_w).astype(BF16); o += sb_w
    sbv_ref[...] = proj(o, o + sb_w).astype(BF16); o += sb_w
    gq_ref[...] = (proj(o, o + kw) * (GLA_DK ** -0.5)).astype(BF16); o += kw
    gk_ref[...] = proj(o, o + kw).astype(BF16); o += kw
    gv_ref[...] = proj(o, o + vw).astype(BF16); o += vw
    r = proj(o, o + vw)
    gr_ref[...] = (r * _sigmoid(r)).astype(BF16)

    code = _dot(xn, wa_ref[...])
    a_pre = _dot(code.astype(BF16), wup_ref[...]) + balpha_ref[...]
    log_a = _log_sigmoid(a_pre) * (1.0 / GLA_TAU)
    la_hi = log_a.astype(BF16)
    la_lo = (log_a - la_hi.astype(F32)).astype(BF16)
    tri = tri_ref[...]
    cum_ref[...] = _dot(tri, la_hi) + _dot(tri, la_lo)

    d = sgs_ref.shape[-1]
    sgs_ref[...] = _sigmoid(_dot(xn, wgate_ref[:, :d]) + bgate_ref[0:1, :]).astype(BF16)
    sgg_ref[...] = _sigmoid(_dot(xn, wgate_ref[:, d:]) + bgate_ref[1:2, :]).astype(BF16)


def _in_proj(x2, norm_g, w_in, b_gate, w_alpha_up, b_alpha):
    t, d = x2.shape
    sb_w = SB_HEADS * SB_DH
    kw = GLA_HEADS * GLA_DK
    vw = GLA_HEADS * GLA_DV
    n_qkv = 3 * sb_w + 2 * kw + 2 * vw
    wqkv = w_in[:, :n_qkv].astype(BF16)
    wa = jnp.pad(w_in[:, n_qkv:n_qkv + GLA_RANK], ((0, 0), (0, LANES - GLA_RANK))).astype(BF16)
    wgate = w_in[:, n_qkv + GLA_RANK:].astype(BF16)
    wup = jnp.pad(w_alpha_up, ((0, LANES - GLA_RANK), (0, 0))).astype(BF16)
    row = lax.broadcasted_iota(jnp.int32, (TM, TM), 0)
    col = lax.broadcasted_iota(jnp.int32, (TM, TM), 1)
    tri = ((row // GLA_CHUNK == col // GLA_CHUNK) & (col <= row)).astype(BF16)

    tok = lambda w: pl.BlockSpec((TM, w), lambda i: (i, 0))
    out_shapes = [jax.ShapeDtypeStruct((t, w), BF16) for w in (sb_w, sb_w, sb_w, kw, kw, vw, vw)]
    out_shapes += [jax.ShapeDtypeStruct((t, kw), F32),
                   jax.ShapeDtypeStruct((t, d), BF16), jax.ShapeDtypeStruct((t, d), BF16)]
    out_specs = [tok(s.shape[1]) for s in out_shapes]
    return pl.pallas_call(
        _in_proj_kernel,
        grid=(t // TM,),
        in_specs=[tok(d), _const_spec((1, d)), _const_spec(wqkv.shape), _const_spec(wa.shape),
                  _const_spec(wup.shape), _const_spec((1, kw)), _const_spec(wgate.shape),
                  _const_spec((2, d)), _const_spec((TM, TM))],
        out_specs=out_specs,
        out_shape=out_shapes,
        compiler_params=pltpu.CompilerParams(dimension_semantics=("arbitrary",),
                                             vmem_limit_bytes=VMEM_LIMIT),
        name="in_proj",
    )(x2, norm_g.reshape(1, d), wqkv, wa, wup, b_alpha.reshape(1, kw), wgate, b_gate, tri)


def _neg_abs(z):
    bits = lax.bitcast_convert_type(z, jnp.uint32) | jnp.uint32(0x80000000)
    return lax.bitcast_convert_type(bits, F32)


def _sb_kernel(qtab_ref, ktab_ref, q_ref, k_ref, v_ref, tri_ref, bias_ref, o_ref,
               z_buf, sp_buf, w_buf, carry_ref, acc_ref, *, n_tiles):
    lane = lax.broadcasted_iota(jnp.int32, (1, LANES), 1)
    hmasks = [(lane >= hh * SB_DH) & (lane < (hh + 1) * SB_DH) for hh in range(2)]
    zero = jnp.zeros((), BF16)
    z_buf[...] = jnp.zeros_like(z_buf)
    sp_buf[...] = jnp.zeros_like(sp_buf)
    w_buf[...] = jnp.zeros_like(w_buf)
    carry_ref[...] = jnp.zeros_like(carry_ref)
    acc_ref[...] = jnp.zeros_like(acc_ref)

    def rows(tile_idx, size):
        return pl.ds(pl.multiple_of(tile_idx * size, size), size)

    def body(i, _):
        slot = i & 1
        other = 1 - slot

        qc = qtab_ref[i]
        kc = ktab_ref[i]
        vt = v_ref[0, rows(kc, SB_TK), :]
        pv = (_dot(w_buf[slot, 0], jnp.where(hmasks[0], vt, zero))
              + _dot(w_buf[slot, 1], jnp.where(hmasks[1], vt, zero)))
        acc = jnp.where(kc == qc, 0.0, acc_ref[...]) + pv
        acc_ref[...] = acc
        o_ref[0, rows(qc, SB_TQ), :] = acc.astype(BF16)

        first_b = ktab_ref[i + 1] == qtab_ref[i + 1]
        neg_tri = tri_ref[...]
        for hh in range(2):
            x = _dot(sp_buf[other, hh], neg_tri)
            carry = jnp.where(first_b, 0.0, carry_ref[hh])
            lw = z_buf[other, hh] + x + jnp.concatenate([carry] * (SB_TK // LANES), axis=1)
            w_buf[other, hh] = jnp.exp2(lw).astype(BF16)
            carry_ref[hh] = carry + jnp.broadcast_to(x[:, 0:1], (SB_TQ, LANES))

        qa = qtab_ref[i + 2]
        ka = ktab_ref[i + 2]
        q = q_ref[0, rows(qa, SB_TQ), :]
        kt = k_ref[0, rows(ka, SB_TK), :]
        bias = bias_ref[(ka == qa).astype(jnp.int32)]
        for hh in range(2):
            z = _nt_dot(jnp.where(hmasks[hh], q, zero), kt) + bias
            sp = jnp.maximum(z, 0.0) + jnp.log(1.0 + jnp.exp2(_neg_abs(z))) * LOG2_E
            z_buf[slot, hh] = z
            sp_buf[slot, hh] = sp.astype(BF16)
        return 0

    lax.fori_loop(0, n_tiles + 2, body, 0)


def _sb_attn(q, k, v):
    b, s, w = q.shape
    nq = s // SB_TQ
    tiles = [(0, 0)] * 2 + [(qi, kj) for qi in range(nq) for kj in range(qi, -1, -1)] + [(0, 0)] * 2
    n_tiles = len(tiles) - 4
    qtab = jnp.asarray([t[0] for t in tiles], jnp.int32)
    ktab = jnp.asarray([t[1] for t in tiles], jnp.int32)
    row = lax.broadcasted_iota(jnp.int32, (SB_TK, SB_TK), 0)
    col = lax.broadcasted_iota(jnp.int32, (SB_TK, SB_TK), 1)
    tri = -(row >= col).astype(BF16)
    bias = jnp.stack([jnp.zeros((SB_TQ, SB_TK), F32),
                      jnp.where(col < row, 0.0, -1e30).astype(F32)])
    seq_spec = pl.BlockSpec((1, s, LANES), lambda bi, p, *_: (bi, 0, p))
    grid_spec = pltpu.PrefetchScalarGridSpec(
        num_scalar_prefetch=2,
        grid=(b, w // LANES),
        in_specs=[seq_spec, seq_spec, seq_spec,
                  pl.BlockSpec((SB_TK, SB_TK), lambda bi, p, *_: (0, 0)),
                  pl.BlockSpec((2, SB_TQ, SB_TK), lambda bi, p, *_: (0, 0, 0))],
        out_specs=seq_spec,
        scratch_shapes=[pltpu.VMEM((2, 2, SB_TQ, SB_TK), F32),
                        pltpu.VMEM((2, 2, SB_TQ, SB_TK), BF16),
                        pltpu.VMEM((2, 2, SB_TQ, SB_TK), BF16),
                        pltpu.VMEM((2, SB_TQ, LANES), F32),
                        pltpu.VMEM((SB_TQ, LANES), F32)],
    )
    return pl.pallas_call(
        functools.partial(_sb_kernel, n_tiles=n_tiles),
        grid_spec=grid_spec,
        out_shape=jax.ShapeDtypeStruct((b, s, w), BF16),
        compiler_params=pltpu.CompilerParams(dimension_semantics=("arbitrary", "arbitrary"),
                                             vmem_limit_bytes=VMEM_LIMIT),
        name="sb_attn",
    )(qtab, ktab, q, k, v, tri, bias)


def _gla_kernel(q_ref, k_ref, v_ref, cum_ref, r_ref, g_ref, o_ref, st_ref):
    @pl.when(pl.program_id(1) == 0)
    def _():
        st_ref[...] = jnp.zeros_like(st_ref)

    c = GLA_CHUNK
    lane = lax.broadcasted_iota(jnp.int32, (1, LANES), 1)
    row = lax.broadcasted_iota(jnp.int32, (c, c), 0)
    col = lax.broadcasted_iota(jnp.int32, (c, c), 1)
    causal = col <= row
    for n in range(GLA_BLK // c):
        rows = slice(n * c, (n + 1) * c)
        for p in range(GLA_HEADS // 2):
            lanes = slice(p * LANES, (p + 1) * LANES)
            cum = cum_ref[0, rows, lanes]
            cum_last = cum[c - 1:c, :]
            qf = q_ref[0, rows, lanes].astype(F32)
            kf = k_ref[0, rows, lanes].astype(F32)
            q_dec = qf * jnp.exp(cum)
            k_inv = (kf * jnp.exp(-cum)).astype(BF16)
            k_end = kf * jnp.exp(cum_last - cum)
            decay = jnp.exp(cum_last)
            for hh in range(2):
                h = 2 * p + hh
                hmask = (lane >= hh * GLA_DK) & (lane < (hh + 1) * GLA_DK)
                vlanes = slice(h * GLA_DV, (h + 1) * GLA_DV)
                vb = v_ref[0, rows, vlanes]
                qd = jnp.where(hmask, q_dec, 0.0).astype(BF16)
                ke = jnp.where(hmask, k_end, 0.0).astype(BF16)
                att = jnp.where(causal, _nt_dot(qd, k_inv), 0.0)
                st = st_ref[h]
                o = _dot(att.astype(BF16), vb) + _nt_dot(qd, st.astype(BF16))
                vt = vb.astype(F32).T.astype(BF16)
                st_ref[h] = st * decay + _dot(vt, ke)
                o = o * lax.rsqrt(jnp.mean(o * o, axis=-1, keepdims=True) + EPS)
                o = o * g_ref[0:1, vlanes] * r_ref[0, rows, vlanes].astype(F32)
                o_ref[0, rows, vlanes] = o.astype(BF16)


def _gla(gq, gk, gv, cum, r_act, norm_g):
    b, s, kw = gq.shape
    vw = gv.shape[-1]
    spec = lambda w: pl.BlockSpec((1, GLA_BLK, w), lambda bi, j: (bi, j, 0))
    return pl.pallas_call(
        _gla_kernel,
        grid=(b, s // GLA_BLK),
        in_specs=[spec(kw), spec(kw), spec(vw), spec(kw), spec(vw), _const_spec((1, vw))],
        out_specs=spec(vw),
        out_shape=jax.ShapeDtypeStruct((b, s, vw), BF16),
        scratch_shapes=[pltpu.VMEM((GLA_HEADS, GLA_DV, LANES), F32)],
        compiler_params=pltpu.CompilerParams(dimension_semantics=("arbitrary", "arbitrary"),
                                             vmem_limit_bytes=VMEM_LIMIT),
        name="gla",
    )(gq, gk, gv, cum, r_act, norm_g.reshape(1, vw))


def _mix_out_kernel(x_ref, osb_ref, ogla_ref, sgs_ref, sgg_ref, wsb_ref, wgla_ref, wout_ref, h_ref):
    y = (sgs_ref[...].astype(F32) * _dot(osb_ref[...], wsb_ref[...])
         + sgg_ref[...].astype(F32) * _dot(ogla_ref[...], wgla_ref[...]))
    h_ref[...] = x_ref[...] + _dot(y.astype(BF16), wout_ref[...])


def _mix_out(x2, o_sb, o_gla, sg_sb, sg_gla, w_branch_sb, w_branch_gla, w_out):
    t, d = x2.shape
    tok = lambda w: pl.BlockSpec((TM, w), lambda i: (i, 0))
    wsb, wgla, wout = (w.astype(BF16) for w in (w_branch_sb, w_branch_gla, w_out))
    return pl.pallas_call(
        _mix_out_kernel,
        grid=(t // TM,),
        in_specs=[tok(d), tok(o_sb.shape[1]), tok(o_gla.shape[1]), tok(d), tok(d),
                  _const_spec(wsb.shape), _const_spec(wgla.shape), _const_spec(wout.shape)],
        out_specs=tok(d),
        out_shape=jax.ShapeDtypeStruct((t, d), F32),
        compiler_params=pltpu.CompilerParams(dimension_semantics=("arbitrary",),
                                             vmem_limit_bytes=VMEM_LIMIT),
        name="mix_out",
    )(x2, o_sb, o_gla, sg_sb, sg_gla, wsb, wgla, wout)


def _gelu_tanh(x):
    return 0.5 * x * (1.0 + jnp.tanh(0.7978845608028654 * (x + 0.044715 * (x * x * x))))


def _conv_ffn_kernel(h_ref, g_ref, win_ref, cw_ref, cb_ref, wout_ref, gf_ref, o_ref,
                     acc_ref, prev_ref, *, tiles_per_seq, d_ff, final_norm):
    @pl.when(pl.program_id(0) % tiles_per_seq == 0)
    def _():
        prev_ref[...] = jnp.zeros_like(prev_ref)

    h = h_ref[...]
    ms = jnp.mean(h * h, axis=-1, keepdims=True)
    hn = (h * lax.rsqrt(ms + EPS) * g_ref[...]).astype(BF16)
    tm = h.shape[0]
    for ci in range(d_ff // FF_CK):
        cols = slice(ci * FF_CK, (ci + 1) * FF_CK)
        a = _dot(hn, win_ref[:, cols])
        gate = _dot(hn, win_ref[:, d_ff + ci * FF_CK:d_ff + (ci + 1) * FF_CK])
        ext = jnp.concatenate([prev_ref[:, cols], a], axis=0)
        prev_ref[:, cols] = a[tm - SUBLANES:, :]
        a1 = pltpu.roll(ext, 1, 0)[SUBLANES:, :]
        a2 = pltpu.roll(ext, 2, 0)[SUBLANES:, :]
        conv = (cb_ref[0:1, cols] + a2 * cw_ref[0:1, cols] + a1 * cw_ref[1:2, cols]
                + a * cw_ref[2:3, cols])
        act = (_gelu_tanh(conv) * gate).astype(BF16)
        part = _dot(act, wout_ref[cols, :])
        if ci == 0:
            acc_ref[...] = part
        else:
            acc_ref[...] += part
    h2 = h + acc_ref[...]
    if final_norm:
        ms2 = jnp.mean(h2 * h2, axis=-1, keepdims=True)
        h2 = h2 * lax.rsqrt(ms2 + EPS) * gf_ref[...]
    o_ref[...] = h2


def _conv_ffn(h2d, seq_len, norm_g, w_ffn_in, conv_w, conv_b, w_ffn_out, norm_final_g, final_norm):
    t, d = h2d.shape
    d_ff = w_ffn_out.shape[0]
    tok = pl.BlockSpec((TM, d), lambda i: (i, 0))
    win = w_ffn_in.astype(BF16)
    wout = w_ffn_out.astype(BF16)
    kern = functools.partial(_conv_ffn_kernel, tiles_per_seq=seq_len // TM, d_ff=d_ff,
                             final_norm=final_norm)
    return pl.pallas_call(
        kern,
        grid=(t // TM,),
        in_specs=[tok, _const_spec((1, d)), _const_spec(win.shape), _const_spec(conv_w.shape),
                  _const_spec((1, d_ff)), _const_spec(wout.shape), _const_spec((1, d))],
        out_specs=tok,
        out_shape=jax.ShapeDtypeStruct((t, d), F32),
        scratch_shapes=[pltpu.VMEM((TM, d), F32), pltpu.VMEM((SUBLANES, d_ff), F32)],
        compiler_params=pltpu.CompilerParams(dimension_semantics=("arbitrary",),
                                             vmem_limit_bytes=VMEM_LIMIT),
        name="conv_ffn",
    )(h2d, norm_g.reshape(1, d), win, conv_w, conv_b.reshape(1, d_ff), wout,
      norm_final_g.reshape(1, d))


def kernel(x, norm_mix_g, w_in, b_gate, w_alpha_up, b_alpha, gla_norm_g, w_branch_sb, w_branch_gla,
           w_out, norm_ffn_g, w_ffn_in, conv_w, conv_b, w_ffn_out, norm_final_g):
    b, s, d = x.shape
    depth = w_in.shape[0]
    assert s % TM == 0 and s % SB_TQ == 0 and s % GLA_BLK == 0 and (b * s) % TM == 0
    assert w_ffn_out.shape[1] % FF_CK == 0 and conv_w.shape[1] == CONV_WIDTH
    h = x.reshape(b * s, d)
    for i in range(depth):
        (sbq, sbk, sbv, gq, gk, gv, r_act, cum, sg_sb, sg_gla) = _in_proj(
            h, norm_mix_g[i], w_in[i], b_gate[i], w_alpha_up[i], b_alpha[i])
        seq = lambda a: a.reshape(b, s, a.shape[-1])
        o_sb = _sb_attn(seq(sbq), seq(sbk), seq(sbv)).reshape(b * s, -1)
        o_gla = _gla(seq(gq), seq(gk), seq(gv), seq(cum), seq(r_act), gla_norm_g[i])
        h = _mix_out(h, o_sb, o_gla.reshape(b * s, -1), sg_sb, sg_gla,
                     w_branch_sb[i], w_branch_gla[i], w_out[i])
        h = _conv_ffn(h, s, norm_ffn_g[i], w_ffn_in[i], conv_w[i], conv_b[i], w_ffn_out[i],
                      norm_final_g, final_norm=(i == depth - 1))
    return h.reshape(b, s, d)
```

```python
import functools

import jax
import jax.numpy as jnp
from jax import lax
from jax.experimental import pallas as pl
from jax.experimental.pallas import tpu as pltpu

F32 = jnp.float32
BF16 = jnp.bfloat16

EPS = 1e-6
LOG2_E = 1.4426950408889634
SB_HEADS = 8
SB_DH = 64
GLA_HEADS = 4
GLA_DK = 64
GLA_DV = 128
GLA_RANK = 16
GLA_TAU = 16.0
GLA_CHUNK = 64
CONV_WIDTH = 3

LANES = 128
SUBLANES = 8
VMEM_LIMIT = 56 * 1024 * 1024

TM = 512
SB_TQ = 256
SB_TK = 256
GLA_BLK = 512
FF_CK = 256


def _nt_dot(a, b):
    return lax.dot_general(a, b, (((1,), (1,)), ((), ())), preferred_element_type=F32)


def _dot(a, b):
    return jnp.dot(a, b, preferred_element_type=F32)


def _log_sigmoid(z):
    return jnp.minimum(z, 0.0) - jnp.log(1.0 + jnp.exp(-jnp.abs(z)))


def _sigmoid(z):
    return 1.0 / (1.0 + jnp.exp(-z))


def _const_spec(shape):
    return pl.BlockSpec(shape, lambda *_: (0,) * len(shape))


def _in_proj_kernel(x_ref, g_ref, wqkv_ref, wa_ref, wup_ref, balpha_ref, wgate_ref, bgate_ref,
                    tri_ref, sbq_ref, sbk_ref, sbv_ref, gq_ref, gk_ref, gv_ref, gr_ref, cum_ref,
                    sgs_ref, sgg_ref):
    x = x_ref[...]
    ms = jnp.mean(x * x, axis=-1, keepdims=True)
    xn = (x * lax.rsqrt(ms + EPS) * g_ref[...]).astype(BF16)

    def proj(lo, hi):
        return _dot(xn, wqkv_ref[:, lo:hi])

    sb_w = SB_HEADS * SB_DH
    kw = GLA_HEADS * GLA_DK
    vw = GLA_HEADS * GLA_DV
    code = _dot(xn, wa_ref[...])
    o = 0
    sbq_ref[...] = (proj(o, o + sb_w) * (SB_DH ** -0.5 * LOG2_E)).astype(BF16); o += sb_w
    a_pre = _dot(code.astype(BF16), wup_ref[...]) + balpha_ref[...]
    sbk_ref[...] = proj(o, o + sb_w).astype(BF16); o += sb_w
    sbv_ref[...] = proj(o, o + sb_w).astype(BF16); o += sb_w
    log_a = _log_sigmoid(a_pre) * (1.0 / GLA_TAU)
    la_hi = log_a.astype(BF16)
    la_lo = (log_a - la_hi.astype(F32)).astype(BF16)
    tri = tri_ref[...]
    cum_ref[...] = _dot(tri, la_hi) + _dot(tri, la_lo)
    gq_ref[...] = (proj(o, o + kw) * (GLA_DK ** -0.5)).astype(BF16); o += kw
    gk_ref[...] = proj(o, o + kw).astype(BF16); o += kw
    gv_ref[...] = proj(o, o + vw).astype(BF16); o += vw
    r = proj(o, o + vw)
    gr_ref[...] = (r * _sigmoid(r)).astype(BF16)

    d = sgs_ref.shape[-1]
    sgs_ref[...] = _sigmoid(_dot(xn, wgate_ref[:, :d]) + bgate_ref[0:1, :]).astype(BF16)
    sgg_ref[...] = _sigmoid(_dot(xn, wgate_ref[:, d:]) + bgate_ref[1:2, :]).astype(BF16)


def _in_proj(x2, norm_g, w_in, b_gate, w_alpha_up, b_alpha):
    t, d = x2.shape
    sb_w = SB_HEADS * SB_DH
    kw = GLA_HEADS * GLA_DK
    vw = GLA_HEADS * GLA_DV
    n_qkv = 3 * sb_w + 2 * kw + 2 * vw
    wqkv = w_in[:, :n_qkv].astype(BF16)
    wa = jnp.pad(w_in[:, n_qkv:n_qkv + GLA_RANK], ((0, 0), (0, LANES - GLA_RANK))).astype(BF16)
    wgate = w_in[:, n_qkv + GLA_RANK:].astype(BF16)
    wup = jnp.pad(w_alpha_up, ((0, LANES - GLA_RANK), (0, 0))).astype(BF16)
    row = lax.broadcasted_iota(jnp.int32, (TM, TM), 0)
    col = lax.broadcasted_iota(jnp.int32, (TM, TM), 1)
    tri = ((row // GLA_CHUNK == col // GLA_CHUNK) & (col <= row)).astype(BF16)

    tok = lambda w: pl.BlockSpec((TM, w), lambda i: (i, 0))
    out_shapes = [jax.ShapeDtypeStruct((t, w), BF16) for w in (sb_w, sb_w, sb_w, kw, kw, vw, vw)]
    out_shapes += [jax.ShapeDtypeStruct((t, kw), F32),
                   jax.ShapeDtypeStruct((t, d), BF16), jax.ShapeDtypeStruct((t, d), BF16)]
    out_specs = [tok(s.shape[1]) for s in out_shapes]
    return pl.pallas_call(
        _in_proj_kernel,
        grid=(t // TM,),
        in_specs=[tok(d), _const_spec((1, d)), _const_spec(wqkv.shape), _const_spec(wa.shape),
                  _const_spec(wup.shape), _const_spec((1, kw)), _const_spec(wgate.shape),
                  _const_spec((2, d)), _const_spec((TM, TM))],
        out_specs=out_specs,
        out_shape=out_shapes,
        compiler_params=pltpu.CompilerParams(dimension_semantics=("arbitrary",),
                                             vmem_limit_bytes=VMEM_LIMIT),
        name="in_proj",
    )(x2, norm_g.reshape(1, d), wqkv, wa, wup, b_alpha.reshape(1, kw), wgate, b_gate, tri)


SB_STAGES = 4


def _sb_kernel(qtab_ref, ktab_ref, atab_ref, q_ref, k_ref, v_ref, tri_ref, bias_ref, o_ref,
               qm_buf, vm_buf, z_buf, sp_buf, lw_buf, carry_ref, acc_ref, *, n_tiles):
    lane = lax.broadcasted_iota(jnp.int32, (1, LANES), 1)
    zero = jnp.zeros((), BF16)
    for hh in range(2):
        hmask = (lane >= hh * SB_DH) & (lane < (hh + 1) * SB_DH)
        qm_buf[hh] = jnp.where(hmask, q_ref[0], zero)
        vm_buf[hh] = jnp.where(hmask, v_ref[0], zero)
    z_buf[...] = jnp.zeros_like(z_buf)
    sp_buf[...] = jnp.zeros_like(sp_buf)
    lw_buf[...] = jnp.zeros_like(lw_buf)
    carry_ref[...] = jnp.zeros_like(carry_ref)
    acc_ref[...] = jnp.zeros_like(acc_ref)

    def rows(tile_idx, size):
        return pl.ds(pl.multiple_of(tile_idx * size, size), size)

    def body(i, _):
        q4 = qtab_ref[i]
        k4 = ktab_ref[i]
        a4 = atab_ref[i]
        p4 = (i + 1) & 1
        pv = (_dot(lw_buf[p4, 0], vm_buf[0, rows(k4, SB_TK), :])
              + _dot(lw_buf[p4, 1], vm_buf[1, rows(k4, SB_TK), :]))
        acc = acc_ref[a4] + pv
        acc_ref[a4] = acc
        o_ref[0, rows(q4, SB_TQ), :] = acc.astype(BF16)

        a3 = atab_ref[i + 1]
        p3 = i & 1
        z3 = (i + 2) & 3
        neg_tri = tri_ref[...]
        for hh in range(2):
            x = _dot(sp_buf[p3, hh], neg_tri)
            carry = carry_ref[a3, hh]
            lw = z_buf[z3, hh] + x + jnp.concatenate([carry] * (SB_TK // LANES), axis=1)
            lw_buf[p3, hh] = jnp.exp2(lw).astype(BF16)
            carry_ref[a3, hh] = carry + jnp.broadcast_to(x[:, 0:1], (SB_TQ, LANES))

        p2 = (i + 1) & 1
        z2 = (i + 3) & 3
        for hh in range(2):
            z = z_buf[z2, hh]
            sp = jnp.maximum(z, 0.0) + jnp.log(1.0 + jnp.exp2(-jnp.abs(z))) * LOG2_E
            sp_buf[p2, hh] = sp.astype(BF16)

        q1 = qtab_ref[i + 3]
        k1 = ktab_ref[i + 3]
        z1 = i & 3
        kt = k_ref[0, rows(k1, SB_TK), :]
        bias = bias_ref[(k1 == q1).astype(jnp.int32)]
        for hh in range(2):
            z_buf[z1, hh] = _nt_dot(qm_buf[hh, rows(q1, SB_TQ), :], kt) + bias
        return 0

    lax.fori_loop(0, n_tiles + SB_STAGES - 1, body, 0)


def _sb_attn(q, k, v):
    b, s, w = q.shape
    nq = s // SB_TQ
    fill = [(0, 0, nq)] * (SB_STAGES - 1)
    tiles = fill + [(qi, kj, qi) for qi in range(nq) for kj in range(qi, -1, -1)] + fill
    n_tiles = len(tiles) - 2 * len(fill)
    qtab, ktab, atab = (jnp.asarray([t[c] for t in tiles], jnp.int32) for c in range(3))
    row = lax.broadcasted_iota(jnp.int32, (SB_TK, SB_TK), 0)
    col = lax.broadcasted_iota(jnp.int32, (SB_TK, SB_TK), 1)
    tri = -(row >= col).astype(BF16)
    bias = jnp.stack([jnp.zeros((SB_TQ, SB_TK), F32),
                      jnp.where(col < row, 0.0, -1e30).astype(F32)])
    seq_spec = pl.BlockSpec((1, s, LANES), lambda bi, p, *_: (bi, 0, p))
    grid_spec = pltpu.PrefetchScalarGridSpec(
        num_scalar_prefetch=3,
        grid=(b, w // LANES),
        in_specs=[seq_spec, seq_spec, seq_spec,
                  pl.BlockSpec((SB_TK, SB_TK), lambda bi, p, *_: (0, 0)),
                  pl.BlockSpec((2, SB_TQ, SB_TK), lambda bi, p, *_: (0, 0, 0))],
        out_specs=seq_spec,
        scratch_shapes=[pltpu.VMEM((2, s, LANES), BF16),
                        pltpu.VMEM((2, s, LANES), BF16),
                        pltpu.VMEM((4, 2, SB_TQ, SB_TK), F32),
                        pltpu.VMEM((2, 2, SB_TQ, SB_TK), BF16),
                        pltpu.VMEM((2, 2, SB_TQ, SB_TK), BF16),
                        pltpu.VMEM((nq + 1, 2, SB_TQ, LANES), F32),
                        pltpu.VMEM((nq + 1, SB_TQ, LANES), F32)],
    )
    return pl.pallas_call(
        functools.partial(_sb_kernel, n_tiles=n_tiles),
        grid_spec=grid_spec,
        out_shape=jax.ShapeDtypeStruct((b, s, w), BF16),
        compiler_params=pltpu.CompilerParams(dimension_semantics=("arbitrary", "arbitrary"),
                                             vmem_limit_bytes=VMEM_LIMIT),
        name="sb_attn",
    )(qtab, ktab, atab, q, k, v, tri, bias)


def _gla_kernel(q_ref, k_ref, v_ref, cum_ref, r_ref, g_ref, o_ref, st_ref):
    @pl.when(pl.program_id(1) == 0)
    def _():
        st_ref[...] = jnp.zeros_like(st_ref)

    c = GLA_CHUNK
    lane = lax.broadcasted_iota(jnp.int32, (1, LANES), 1)
    row = lax.broadcasted_iota(jnp.int32, (c, c), 0)
    col = lax.broadcasted_iota(jnp.int32, (c, c), 1)
    causal = col <= row
    for n in range(GLA_BLK // c):
        rows = slice(n * c, (n + 1) * c)
        for p in range(GLA_HEADS // 2):
            lanes = slice(p * LANES, (p + 1) * LANES)
            cum = cum_ref[0, rows, lanes]
            cum_last = cum[c - 1:c, :]
            qf = q_ref[0, rows, lanes].astype(F32)
            kf = k_ref[0, rows, lanes].astype(F32)
            q_dec = qf * jnp.exp(cum)
            k_inv = (kf * jnp.exp(-cum)).astype(BF16)
            k_end = kf * jnp.exp(cum_last - cum)
            decay = jnp.exp(cum_last)
            for hh in range(2):
                h = 2 * p + hh
                hmask = (lane >= hh * GLA_DK) & (lane < (hh + 1) * GLA_DK)
                vlanes = slice(h * GLA_DV, (h + 1) * GLA_DV)
                vb = v_ref[0, rows, vlanes]
                qd = jnp.where(hmask, q_dec, 0.0).astype(BF16)
                ke = jnp.where(hmask, k_end, 0.0).astype(BF16)
                att = jnp.where(causal, _nt_dot(qd, k_inv), 0.0)
                st = st_ref[h]
                o = _dot(att.astype(BF16), vb) + _nt_dot(qd, st.astype(BF16))
                vt = vb.astype(F32).T.astype(BF16)
                st_ref[h] = st * decay + _dot(vt, ke)
                o = o * lax.rsqrt(jnp.mean(o * o, axis=-1, keepdims=True) + EPS)
                o = o * g_ref[0:1, vlanes] * r_ref[0, rows, vlanes].astype(F32)
                o_ref[0, rows, vlanes] = o.astype(BF16)


def _gla(gq, gk, gv, cum, r_act, norm_g):
    b, s, kw = gq.shape
    vw = gv.shape[-1]
    spec = lambda w: pl.BlockSpec((1, GLA_BLK, w), lambda bi, j: (bi, j, 0))
    return pl.pallas_call(
        _gla_kernel,
        grid=(b, s // GLA_BLK),
        in_specs=[spec(kw), spec(kw), spec(vw), spec(kw), spec(vw), _const_spec((1, vw))],
        out_specs=spec(vw),
        out_shape=jax.ShapeDtypeStruct((b, s, vw), BF16),
        scratch_shapes=[pltpu.VMEM((GLA_HEADS, GLA_DV, LANES), F32)],
        compiler_params=pltpu.CompilerParams(dimension_semantics=("arbitrary", "arbitrary"),
                                             vmem_limit_bytes=VMEM_LIMIT),
        name="gla",
    )(gq, gk, gv, cum, r_act, norm_g.reshape(1, vw))


def _mix_out_kernel(x_ref, osb_ref, ogla_ref, sgs_ref, sgg_ref, wsb_ref, wgla_ref, wout_ref, h_ref):
    y = (sgs_ref[...].astype(F32) * _dot(osb_ref[...], wsb_ref[...])
         + sgg_ref[...].astype(F32) * _dot(ogla_ref[...], wgla_ref[...]))
    h_ref[...] = x_ref[...] + _dot(y.astype(BF16), wout_ref[...])


def _mix_out(x2, o_sb, o_gla, sg_sb, sg_gla, w_branch_sb, w_branch_gla, w_out):
    t, d = x2.shape
    tok = lambda w: pl.BlockSpec((TM, w), lambda i: (i, 0))
    wsb, wgla, wout = (w.astype(BF16) for w in (w_branch_sb, w_branch_gla, w_out))
    return pl.pallas_call(
        _mix_out_kernel,
        grid=(t // TM,),
        in_specs=[tok(d), tok(o_sb.shape[1]), tok(o_gla.shape[1]), tok(d), tok(d),
                  _const_spec(wsb.shape), _const_spec(wgla.shape), _const_spec(wout.shape)],
        out_specs=tok(d),
        out_shape=jax.ShapeDtypeStruct((t, d), F32),
        compiler_params=pltpu.CompilerParams(dimension_semantics=("arbitrary",),
                                             vmem_limit_bytes=VMEM_LIMIT),
        name="mix_out",
    )(x2, o_sb, o_gla, sg_sb, sg_gla, wsb, wgla, wout)


GELU_C0 = 0.7978845608028654
GELU_C1 = 0.7978845608028654 * 0.044715


def _conv_ffn_kernel(h_ref, g_ref, wa_ref, wg_ref, cw_ref, cb_ref, wout_ref, gf_ref, o_ref,
                     hn_ref, act_ref, acc_ref, prev_ref, *, tiles_per_seq, final_norm):
    n_ck = wa_ref.shape[0]
    tm = h_ref.shape[0]

    @pl.when(pl.program_id(0) % tiles_per_seq == 0)
    def _():
        prev_ref[...] = jnp.zeros_like(prev_ref)

    h = h_ref[...]
    ms = jnp.mean(h * h, axis=-1, keepdims=True)
    hn_ref[...] = (h * lax.rsqrt(ms + EPS) * g_ref[...]).astype(BF16)

    def out_proj(c):
        part = _dot(act_ref[c % 2], wout_ref[c])
        if c == 0:
            acc_ref[...] = part
        else:
            acc_ref[...] += part

    for c in range(n_ck):
        hn = hn_ref[...]
        a = _dot(hn, wa_ref[c])
        gate = _dot(hn, wg_ref[c])
        if c > 0:
            out_proj(c - 1)
        ext = jnp.concatenate([prev_ref[c], a], axis=0)
        prev_ref[c] = a[tm - SUBLANES:, :]
        a1 = pltpu.roll(ext, 1, 0)[SUBLANES:, :]
        a2 = pltpu.roll(ext, 2, 0)[SUBLANES:, :]
        cw = cw_ref[c]
        conv = cb_ref[c] + a2 * cw[0:1, :] + a1 * cw[1:2, :] + a * cw[2:3, :]
        t = jnp.tanh(conv * (GELU_C0 + GELU_C1 * (conv * conv)))
        act_ref[c % 2] = ((conv * gate) * (1.0 + t)).astype(BF16)
    out_proj(n_ck - 1)
    h2 = h + acc_ref[...]
    if final_norm:
        ms2 = jnp.mean(h2 * h2, axis=-1, keepdims=True)
        h2 = h2 * lax.rsqrt(ms2 + EPS) * gf_ref[...]
    o_ref[...] = h2


def _conv_ffn(h2d, seq_len, norm_g, w_ffn_in, conv_w, conv_b, w_ffn_out, norm_final_g, final_norm):
    t, d = h2d.shape
    d_ff = w_ffn_out.shape[0]
    n_ck = d_ff // FF_CK
    tok = pl.BlockSpec((TM, d), lambda i: (i, 0))
    chunked = lambda w: w.reshape(w.shape[0], n_ck, FF_CK).transpose(1, 0, 2)
    wa = chunked(w_ffn_in[:, :d_ff]).astype(BF16)
    wg = chunked(w_ffn_in[:, d_ff:]).astype(BF16)
    wout = (0.5 * w_ffn_out).astype(BF16).reshape(n_ck, FF_CK, d)
    cw = chunked(conv_w)
    cb = chunked(conv_b.reshape(1, d_ff))
    kern = functools.partial(_conv_ffn_kernel, tiles_per_seq=seq_len // TM, final_norm=final_norm)
    return pl.pallas_call(
        kern,
        grid=(t // TM,),
        in_specs=[tok, _const_spec((1, d)), _const_spec(wa.shape), _const_spec(wg.shape),
                  _const_spec(cw.shape), _const_spec(cb.shape), _const_spec(wout.shape),
                  _const_spec((1, d))],
        out_specs=tok,
        out_shape=jax.ShapeDtypeStruct((t, d), F32),
        scratch_shapes=[pltpu.VMEM((TM, d), BF16),
                        pltpu.VMEM((2, TM, FF_CK), BF16),
                        pltpu.VMEM((TM, d), F32),
                        pltpu.VMEM((n_ck, SUBLANES, FF_CK), F32)],
        compiler_params=pltpu.CompilerParams(dimension_semantics=("arbitrary",),
                                             vmem_limit_bytes=VMEM_LIMIT),
        name="conv_ffn",
    )(h2d, norm_g.reshape(1, d), wa, wg, cw, cb, wout, norm_final_g.reshape(1, d))


def kernel(x, norm_mix_g, w_in, b_gate, w_alpha_up, b_alpha, gla_norm_g, w_branch_sb, w_branch_gla,
           w_out, norm_ffn_g, w_ffn_in, conv_w, conv_b, w_ffn_out, norm_final_g):
    b, s, d = x.shape
    depth = w_in.shape[0]
    assert s % TM == 0 and s % SB_TQ == 0 and s % GLA_BLK == 0 and (b * s) % TM == 0
    assert w_ffn_out.shape[1] % FF_CK == 0 and conv_w.shape[1] == CONV_WIDTH
    h = x.reshape(b * s, d)
    for i in range(depth):
        (sbq, sbk, sbv, gq, gk, gv, r_act, cum, sg_sb, sg_gla) = _in_proj(
            h, norm_mix_g[i], w_in[i], b_gate[i], w_alpha_up[i], b_alpha[i])
        seq = lambda a: a.reshape(b, s, a.shape[-1])
        o_sb = _sb_attn(seq(sbq), seq(sbk), seq(sbv)).reshape(b * s, -1)
        o_gla = _gla(seq(gq), seq(gk), seq(gv), seq(cum), seq(r_act), gla_norm_g[i])
        h = _mix_out(h, o_sb, o_gla.reshape(b * s, -1), sg_sb, sg_gla,
                     w_branch_sb[i], w_branch_gla[i], w_out[i])
        h = _conv_ffn(h, s, norm_ffn_g[i], w_ffn_in[i], conv_w[i], conv_b[i], w_ffn_out[i],
                      norm_final_g, final_norm=(i == depth - 1))
    return h.reshape(b, s, d)
```

```python
import functools

import jax
import jax.numpy as jnp
from jax import lax
from jax.experimental import pallas as pl
from jax.experimental.pallas import tpu as pltpu

F32 = jnp.float32
BF16 = jnp.bfloat16

EPS = 1e-6
LOG2_E = 1.4426950408889634
SB_HEADS = 8
SB_DH = 64
GLA_HEADS = 4
GLA_DK = 64
GLA_DV = 128
GLA_RANK = 16
GLA_TAU = 16.0
GLA_CHUNK = 64
CONV_WIDTH = 3

LANES = 128
SUBLANES = 8
VMEM_LIMIT = 56 * 1024 * 1024

TM = 512
SB_TQ = 256
SB_TK = 256
GLA_BLK = 512
FF_CK = 256
FF_RB = 256


def _nt_dot(a, b):
    return lax.dot_general(a, b, (((1,), (1,)), ((), ())), preferred_element_type=F32)


def _dot(a, b):
    return jnp.dot(a, b, preferred_element_type=F32)


def _log_sigmoid(z):
    return jnp.minimum(z, 0.0) - jnp.log(1.0 + jnp.exp(-jnp.abs(z)))


def _sigmoid(z):
    return 1.0 / (1.0 + jnp.exp(-z))


def _const_spec(shape):
    return pl.BlockSpec(shape, lambda *_: (0,) * len(shape))


def _in_proj_kernel(x_ref, g_ref, wqkv_ref, wa_ref, wup_ref, balpha_ref, wgate_ref, bgate_ref,
                    tri_ref, sbq_ref, sbk_ref, sbv_ref, gq_ref, gk_ref, gv_ref, gr_ref, cum_ref,
                    sgs_ref, sgg_ref):
    x = x_ref[...]
    ms = jnp.mean(x * x, axis=-1, keepdims=True)
    xn = (x * lax.rsqrt(ms + EPS) * g_ref[...]).astype(BF16)

    def proj(lo, hi):
        return _dot(xn, wqkv_ref[:, lo:hi])

    sb_w = SB_HEADS * SB_DH
    kw = GLA_HEADS * GLA_DK
    vw = GLA_HEADS * GLA_DV
    code = _dot(xn, wa_ref[...])
    o = 0
    sbq_ref[...] = (proj(o, o + sb_w) * (SB_DH ** -0.5 * LOG2_E)).astype(BF16); o += sb_w
    a_pre = _dot(code.astype(BF16), wup_ref[...]) + balpha_ref[...]
    sbk_ref[...] = proj(o, o + sb_w).astype(BF16); o += sb_w
    sbv_ref[...] = proj(o, o + sb_w).astype(BF16); o += sb_w
    log_a = _log_sigmoid(a_pre) * (1.0 / GLA_TAU)
    la_hi = log_a.astype(BF16)
    la_lo = (log_a - la_hi.astype(F32)).astype(BF16)
    tri = tri_ref[...]
    cum_ref[...] = _dot(tri, la_hi) + _dot(tri, la_lo)
    gq_ref[...] = (proj(o, o + kw) * (GLA_DK ** -0.5)).astype(BF16); o += kw
    gk_ref[...] = proj(o, o + kw).astype(BF16); o += kw
    gv_ref[...] = proj(o, o + vw).astype(BF16); o += vw
    r = proj(o, o + vw)
    gr_ref[...] = (r * _sigmoid(r)).astype(BF16)

    d = sgs_ref.shape[-1]
    sgs_ref[...] = _sigmoid(_dot(xn, wgate_ref[:, :d]) + bgate_ref[0:1, :]).astype(BF16)
    sgg_ref[...] = _sigmoid(_dot(xn, wgate_ref[:, d:]) + bgate_ref[1:2, :]).astype(BF16)


def _in_proj(x2, norm_g, w_in, b_gate, w_alpha_up, b_alpha):
    t, d = x2.shape
    sb_w = SB_HEADS * SB_DH
    kw = GLA_HEADS * GLA_DK
    vw = GLA_HEADS * GLA_DV
    n_qkv = 3 * sb_w + 2 * kw + 2 * vw
    wqkv = w_in[:, :n_qkv].astype(BF16)
    wa = jnp.pad(w_in[:, n_qkv:n_qkv + GLA_RANK], ((0, 0), (0, LANES - GLA_RANK))).astype(BF16)
    wgate = w_in[:, n_qkv + GLA_RANK:].astype(BF16)
    wup = jnp.pad(w_alpha_up, ((0, LANES - GLA_RANK), (0, 0))).astype(BF16)
    row = lax.broadcasted_iota(jnp.int32, (TM, TM), 0)
    col = lax.broadcasted_iota(jnp.int32, (TM, TM), 1)
    tri = ((row // GLA_CHUNK == col // GLA_CHUNK) & (col <= row)).astype(BF16)

    tok = lambda w: pl.BlockSpec((TM, w), lambda i: (i, 0))
    out_shapes = [jax.ShapeDtypeStruct((t, w), BF16) for w in (sb_w, sb_w, sb_w, kw, kw, vw, vw)]
    out_shapes += [jax.ShapeDtypeStruct((t, kw), F32),
                   jax.ShapeDtypeStruct((t, d), BF16), jax.ShapeDtypeStruct((t, d), BF16)]
    out_specs = [tok(s.shape[1]) for s in out_shapes]
    return pl.pallas_call(
        _in_proj_kernel,
        grid=(t // TM,),
        in_specs=[tok(d), _const_spec((1, d)), _const_spec(wqkv.shape), _const_spec(wa.shape),
                  _const_spec(wup.shape), _const_spec((1, kw)), _const_spec(wgate.shape),
                  _const_spec((2, d)), _const_spec((TM, TM))],
        out_specs=out_specs,
        out_shape=out_shapes,
        compiler_params=pltpu.CompilerParams(dimension_semantics=("arbitrary",),
                                             vmem_limit_bytes=VMEM_LIMIT),
        name="in_proj",
    )(x2, norm_g.reshape(1, d), wqkv, wa, wup, b_alpha.reshape(1, kw), wgate, b_gate, tri)


SB_STAGES = 4


def _sb_kernel(qtab_ref, ktab_ref, atab_ref, q_ref, k_ref, v_ref, tri_ref, bias_ref, o_ref,
               qm_buf, vm_buf, z_buf, sp_buf, w_buf, carry_ref, acc_ref, *, n_tiles):
    lane = lax.broadcasted_iota(jnp.int32, (1, LANES), 1)
    zero = jnp.zeros((), BF16)
    for hh in range(2):
        hmask = (lane >= hh * SB_DH) & (lane < (hh + 1) * SB_DH)
        qm_buf[hh] = jnp.where(hmask, q_ref[0], zero)
        vm_buf[hh] = jnp.where(hmask, v_ref[0], zero)
    z_buf[2] = jnp.zeros(z_buf.shape[1:], F32)
    z_buf[3] = jnp.zeros(z_buf.shape[1:], F32)
    sp_buf[0] = jnp.zeros(sp_buf.shape[1:], BF16)
    w_buf[1] = jnp.zeros(w_buf.shape[1:], BF16)
    carry_ref[...] = jnp.zeros_like(carry_ref)
    acc_ref[...] = jnp.zeros_like(acc_ref)

    def rows(tile_idx, size):
        return pl.ds(pl.multiple_of(tile_idx * size, size), size)

    def body(i, _):
        q4 = qtab_ref[i]
        k4 = ktab_ref[i]
        a4 = atab_ref[i]
        p4 = (i + 1) & 1
        pv = (_dot(w_buf[p4, 0], vm_buf[0, rows(k4, SB_TK), :])
              + _dot(w_buf[p4, 1], vm_buf[1, rows(k4, SB_TK), :]))
        acc = acc_ref[a4] + pv
        acc_ref[a4] = acc
        o_ref[0, rows(q4, SB_TQ), :] = acc.astype(BF16)

        a3 = atab_ref[i + 1]
        p3 = i & 1
        z3 = (i + 2) & 3
        neg_tri = tri_ref[...]
        for hh in range(2):
            x = _dot(sp_buf[p3, hh], neg_tri)
            carry = carry_ref[a3, hh]
            lw = z_buf[z3, hh] + x + jnp.concatenate([carry] * (SB_TK // LANES), axis=1)
            w_buf[p3, hh] = jnp.exp2(lw).astype(BF16)
            carry_ref[a3, hh] = carry + jnp.broadcast_to(x[:, 0:1], (SB_TQ, LANES))

        p2 = (i + 1) & 1
        z2 = (i + 3) & 3
        for hh in range(2):
            z = z_buf[z2, hh]
            sp = jnp.maximum(z, 0.0) + jnp.log(1.0 + jnp.exp2(-jnp.abs(z))) * LOG2_E
            sp_buf[p2, hh] = sp.astype(BF16)

        q1 = qtab_ref[i + 3]
        k1 = ktab_ref[i + 3]
        z1 = i & 3
        kt = k_ref[0, rows(k1, SB_TK), :]
        bias = bias_ref[(k1 == q1).astype(jnp.int32)]
        for hh in range(2):
            z_buf[z1, hh] = _nt_dot(qm_buf[hh, rows(q1, SB_TQ), :], kt) + bias
        return 0

    lax.fori_loop(0, n_tiles + SB_STAGES - 1, body, 0)


def _sb_attn(q, k, v):
    b, s, w = q.shape
    nq = s // SB_TQ
    fill = [(0, 0, nq)] * (SB_STAGES - 1)
    tiles = fill + [(qi, kj, qi) for qi in range(nq) for kj in range(qi, -1, -1)] + fill
    n_tiles = len(tiles) - 2 * len(fill)
    qtab, ktab, atab = (jnp.asarray([t[c] for t in tiles], jnp.int32) for c in range(3))
    row = lax.broadcasted_iota(jnp.int32, (SB_TK, SB_TK), 0)
    col = lax.broadcasted_iota(jnp.int32, (SB_TK, SB_TK), 1)
    tri = -(row >= col).astype(BF16)
    bias = jnp.stack([jnp.zeros((SB_TQ, SB_TK), F32),
                      jnp.where(col < row, 0.0, -1e30).astype(F32)])
    seq_spec = pl.BlockSpec((1, s, LANES), lambda bi, p, *_: (bi, 0, p))
    grid_spec = pltpu.PrefetchScalarGridSpec(
        num_scalar_prefetch=3,
        grid=(b, w // LANES),
        in_specs=[seq_spec, seq_spec, seq_spec,
                  pl.BlockSpec((SB_TK, SB_TK), lambda bi, p, *_: (0, 0)),
                  pl.BlockSpec((2, SB_TQ, SB_TK), lambda bi, p, *_: (0, 0, 0))],
        out_specs=seq_spec,
        scratch_shapes=[pltpu.VMEM((2, s, LANES), BF16),
                        pltpu.VMEM((2, s, LANES), BF16),
                        pltpu.VMEM((4, 2, SB_TQ, SB_TK), F32),
                        pltpu.VMEM((2, 2, SB_TQ, SB_TK), BF16),
                        pltpu.VMEM((2, 2, SB_TQ, SB_TK), BF16),
                        pltpu.VMEM((nq + 1, 2, SB_TQ, LANES), F32),
                        pltpu.VMEM((nq + 1, SB_TQ, LANES), F32)],
    )
    return pl.pallas_call(
        functools.partial(_sb_kernel, n_tiles=n_tiles),
        grid_spec=grid_spec,
        out_shape=jax.ShapeDtypeStruct((b, s, w), BF16),
        compiler_params=pltpu.CompilerParams(dimension_semantics=("arbitrary", "arbitrary"),
                                             vmem_limit_bytes=VMEM_LIMIT),
        name="sb_attn",
    )(qtab, ktab, atab, q, k, v, tri, bias)


def _gla_kernel(q_ref, k_ref, v_ref, cum_ref, r_ref, g_ref, o_ref, st_ref):
    @pl.when(pl.program_id(1) == 0)
    def _():
        st_ref[...] = jnp.zeros_like(st_ref)

    c = GLA_CHUNK
    lane = lax.broadcasted_iota(jnp.int32, (1, LANES), 1)
    row = lax.broadcasted_iota(jnp.int32, (c, c), 0)
    col = lax.broadcasted_iota(jnp.int32, (c, c), 1)
    causal = col <= row
    for n in range(GLA_BLK // c):
        rows = slice(n * c, (n + 1) * c)
        for p in range(GLA_HEADS // 2):
            lanes = slice(p * LANES, (p + 1) * LANES)
            cum = cum_ref[0, rows, lanes]
            cum_last = cum[c - 1:c, :]
            qf = q_ref[0, rows, lanes].astype(F32)
            kf = k_ref[0, rows, lanes].astype(F32)
            q_dec = qf * jnp.exp(cum)
            k_inv = (kf * jnp.exp(-cum)).astype(BF16)
            k_end = kf * jnp.exp(cum_last - cum)
            decay = jnp.exp(cum_last)
            for hh in range(2):
                h = 2 * p + hh
                hmask = (lane >= hh * GLA_DK) & (lane < (hh + 1) * GLA_DK)
                vlanes = slice(h * GLA_DV, (h + 1) * GLA_DV)
                vb = v_ref[0, rows, vlanes]
                qd = jnp.where(hmask, q_dec, 0.0).astype(BF16)
                ke = jnp.where(hmask, k_end, 0.0).astype(BF16)
                att = jnp.where(causal, _nt_dot(qd, k_inv), 0.0)
                st = st_ref[h]
                o = _dot(att.astype(BF16), vb) + _nt_dot(qd, st.astype(BF16))
                vt = vb.astype(F32).T.astype(BF16)
                st_ref[h] = st * decay + _dot(vt, ke)
                o = o * lax.rsqrt(jnp.mean(o * o, axis=-1, keepdims=True) + EPS)
                o = o * g_ref[0:1, vlanes] * r_ref[0, rows, vlanes].astype(F32)
                o_ref[0, rows, vlanes] = o.astype(BF16)


def _gla(gq, gk, gv, cum, r_act, norm_g):
    b, s, kw = gq.shape
    vw = gv.shape[-1]
    spec = lambda w: pl.BlockSpec((1, GLA_BLK, w), lambda bi, j: (bi, j, 0))
    return pl.pallas_call(
        _gla_kernel,
        grid=(b, s // GLA_BLK),
        in_specs=[spec(kw), spec(kw), spec(vw), spec(kw), spec(vw), _const_spec((1, vw))],
        out_specs=spec(vw),
        out_shape=jax.ShapeDtypeStruct((b, s, vw), BF16),
        scratch_shapes=[pltpu.VMEM((GLA_HEADS, GLA_DV, LANES), F32)],
        compiler_params=pltpu.CompilerParams(dimension_semantics=("arbitrary", "arbitrary"),
                                             vmem_limit_bytes=VMEM_LIMIT),
        name="gla",
    )(gq, gk, gv, cum, r_act, norm_g.reshape(1, vw))


def _mix_out_kernel(x_ref, osb_ref, ogla_ref, sgs_ref, sgg_ref, wsb_ref, wgla_ref, wout_ref, h_ref):
    y = (sgs_ref[...].astype(F32) * _dot(osb_ref[...], wsb_ref[...])
         + sgg_ref[...].astype(F32) * _dot(ogla_ref[...], wgla_ref[...]))
    h_ref[...] = x_ref[...] + _dot(y.astype(BF16), wout_ref[...])


def _mix_out(x2, o_sb, o_gla, sg_sb, sg_gla, w_branch_sb, w_branch_gla, w_out):
    t, d = x2.shape
    tok = lambda w: pl.BlockSpec((TM, w), lambda i: (i, 0))
    wsb, wgla, wout = (w.astype(BF16) for w in (w_branch_sb, w_branch_gla, w_out))
    return pl.pallas_call(
        _mix_out_kernel,
        grid=(t // TM,),
        in_specs=[tok(d), tok(o_sb.shape[1]), tok(o_gla.shape[1]), tok(d), tok(d),
                  _const_spec(wsb.shape), _const_spec(wgla.shape), _const_spec(wout.shape)],
        out_specs=tok(d),
        out_shape=jax.ShapeDtypeStruct((t, d), F32),
        compiler_params=pltpu.CompilerParams(dimension_semantics=("arbitrary",),
                                             vmem_limit_bytes=VMEM_LIMIT),
        name="mix_out",
    )(x2, o_sb, o_gla, sg_sb, sg_gla, wsb, wgla, wout)


GELU_C0 = 0.7978845608028654
GELU_C1 = 0.7978845608028654 * 0.044715


def _conv_ffn_kernel(h_ref, g_ref, wag_ref, cw_ref, cb_ref, wout_ref, gf_ref, o_ref,
                     hn_ref, act_ref, acc_ref, prev_ref, *, tiles_per_seq, final_norm):
    n_ck = wag_ref.shape[0]
    tm = h_ref.shape[0]

    @pl.when(pl.program_id(0) % tiles_per_seq == 0)
    def _():
        prev_ref[...] = jnp.zeros_like(prev_ref)

    h = h_ref[...]
    ms = jnp.mean(h * h, axis=-1, keepdims=True)
    hn_ref[...] = (h * lax.rsqrt(ms + EPS) * g_ref[...]).astype(BF16)

    row_blocks = [slice(r, r + FF_RB) for r in range(0, tm, FF_RB)]

    def out_proj(c):
        for rb in row_blocks:
            part = _dot(act_ref[c % 2, rb, :], wout_ref[c])
            if c == 0:
                acc_ref[rb, :] = part
            else:
                acc_ref[rb, :] += part

    for c in range(n_ck):
        hn = hn_ref[...]
        ag = _dot(hn, wag_ref[c])
        a = ag[:, :FF_CK]
        gate = ag[:, FF_CK:]
        if c > 0:
            out_proj(c - 1)
        ext = jnp.concatenate([prev_ref[c], a], axis=0)
        prev_ref[c] = a[tm - SUBLANES:, :]
        a1 = pltpu.roll(ext, 1, 0)[SUBLANES:, :]
        a2 = pltpu.roll(ext, 2, 0)[SUBLANES:, :]
        cw = cw_ref[c]
        for rb in row_blocks:
            conv = cb_ref[c] + a2[rb] * cw[0:1, :] + a1[rb] * cw[1:2, :] + a[rb] * cw[2:3, :]
            xb = conv.astype(BF16)
            gb = gate[rb].astype(BF16)
            t = jnp.tanh(xb * (GELU_C0 + GELU_C1 * (xb * xb)))
            act_ref[c % 2, rb, :] = (xb * gb) * (1.0 + t)
    out_proj(n_ck - 1)
    h2 = h + acc_ref[...]
    if final_norm:
        ms2 = jnp.mean(h2 * h2, axis=-1, keepdims=True)
        h2 = h2 * lax.rsqrt(ms2 + EPS) * gf_ref[...]
    o_ref[...] = h2


def _conv_ffn(h2d, seq_len, norm_g, w_ffn_in, conv_w, conv_b, w_ffn_out, norm_final_g, final_norm):
    t, d = h2d.shape
    d_ff = w_ffn_out.shape[0]
    n_ck = d_ff // FF_CK
    tok = pl.BlockSpec((TM, d), lambda i: (i, 0))
    chunked = lambda w: w.reshape(w.shape[0], n_ck, FF_CK).transpose(1, 0, 2)
    wag = jnp.concatenate([chunked(w_ffn_in[:, :d_ff]), chunked(w_ffn_in[:, d_ff:])],
                          axis=-1).astype(BF16)
    wout = (0.5 * w_ffn_out).astype(BF16).reshape(n_ck, FF_CK, d)
    cw = chunked(conv_w)
    cb = chunked(conv_b.reshape(1, d_ff))
    kern = functools.partial(_conv_ffn_kernel, tiles_per_seq=seq_len // TM, final_norm=final_norm)
    return pl.pallas_call(
        kern,
        grid=(t // TM,),
        in_specs=[tok, _const_spec((1, d)), _const_spec(wag.shape), _const_spec(cw.shape), _const_spec(cb.shape), _const_spec(wout.shape),
                  _const_spec((1, d))],
        out_specs=tok,
        out_shape=jax.ShapeDtypeStruct((t, d), F32),
        scratch_shapes=[pltpu.VMEM((TM, d), BF16),
                        pltpu.VMEM((2, TM, FF_CK), BF16),
                        pltpu.VMEM((TM, d), F32),
                        pltpu.VMEM((n_ck, SUBLANES, FF_CK), F32)],
        compiler_params=pltpu.CompilerParams(dimension_semantics=("arbitrary",),
                                             vmem_limit_bytes=VMEM_LIMIT),
        name="conv_ffn",
    )(h2d, norm_g.reshape(1, d), wag, cw, cb, wout, norm_final_g.reshape(1, d))


def kernel(x, norm_mix_g, w_in, b_gate, w_alpha_up, b_alpha, gla_norm_g, w_branch_sb, w_branch_gla,
           w_out, norm_ffn_g, w_ffn_in, conv_w, conv_b, w_ffn_out, norm_final_g):
    b, s, d = x.shape
    depth = w_in.shape[0]
    assert s % TM == 0 and s % SB_TQ == 0 and s % GLA_BLK == 0 and (b * s) % TM == 0
    assert w_ffn_out.shape[1] % FF_CK == 0 and conv_w.shape[1] == CONV_WIDTH
    h = x.reshape(b * s, d)
    for i in range(depth):
        (sbq, sbk, sbv, gq, gk, gv, r_act, cum, sg_sb, sg_gla) = _in_proj(
            h, norm_mix_g[i], w_in[i], b_gate[i], w_alpha_up[i], b_alpha[i])
        seq = lambda a: a.reshape(b, s, a.shape[-1])
        o_sb = _sb_attn(seq(sbq), seq(sbk), seq(sbv)).reshape(b * s, -1)
        o_gla = _gla(seq(gq), seq(gk), seq(gv), seq(cum), seq(r_act), gla_norm_g[i])
        h = _mix_out(h, o_sb, o_gla.reshape(b * s, -1), sg_sb, sg_gla,
                     w_branch_sb[i], w_branch_gla[i], w_out[i])
        h = _conv_ffn(h, s, norm_ffn_g[i], w_ffn_in[i], conv_w[i], conv_b[i], w_ffn_out[i],
                      norm_final_g, final_norm=(i == depth - 1))
    return h.reshape(b, s, d)
```

```python
import functools

import jax
import jax.numpy as jnp
from jax import lax
from jax.experimental import pallas as pl
from jax.experimental.pallas import tpu as pltpu

F32 = jnp.float32
BF16 = jnp.bfloat16

EPS = 1e-6
LOG2_E = 1.4426950408889634
SB_HEADS = 8
SB_DH = 64
GLA_HEADS = 4
GLA_DK = 64
GLA_DV = 128
GLA_RANK = 16
GLA_TAU = 16.0
GLA_CHUNK = 64
CONV_WIDTH = 3

LANES = 128
SUBLANES = 8
VMEM_LIMIT = 56 * 1024 * 1024

TM = 512
SB_TQ = 256
SB_TK = 256
GLA_BLK = 512
FF_CK = 256
FF_RB = 256


def _nt_dot(a, b):
    return lax.dot_general(a, b, (((1,), (1,)), ((), ())), preferred_element_type=F32)


def _dot(a, b):
    return jnp.dot(a, b, preferred_element_type=F32)


def _log_sigmoid(z):
    return jnp.minimum(z, 0.0) - jnp.log(1.0 + jnp.exp(-jnp.abs(z)))


def _sigmoid(z):
    return 1.0 / (1.0 + jnp.exp(-z))


def _const_spec(shape):
    return pl.BlockSpec(shape, lambda *_: (0,) * len(shape))


def _in_proj_kernel(x_ref, g_ref, wqkv_ref, wa_ref, wup_ref, balpha_ref, wgate_ref, bgate_ref,
                    tri_ref, sbq_ref, sbk_ref, sbv_ref, gq_ref, gk_ref, gv_ref, gr_ref, cum_ref,
                    sgs_ref, sgg_ref):
    x = x_ref[...]
    ms = jnp.mean(x * x, axis=-1, keepdims=True)
    xn = (x * lax.rsqrt(ms + EPS) * g_ref[...]).astype(BF16)

    def proj(lo, hi):
        return _dot(xn, wqkv_ref[:, lo:hi])

    sb_w = SB_HEADS * SB_DH
    kw = GLA_HEADS * GLA_DK
    vw = GLA_HEADS * GLA_DV
    code = _dot(xn, wa_ref[...])
    o = 0
    sbq_ref[...] = (proj(o, o + sb_w) * (SB_DH ** -0.5 * LOG2_E)).astype(BF16); o += sb_w
    a_pre = _dot(code.astype(BF16), wup_ref[...]) + balpha_ref[...]
    sbk_ref[...] = proj(o, o + sb_w).astype(BF16); o += sb_w
    sbv_ref[...] = proj(o, o + sb_w).astype(BF16); o += sb_w
    log_a = _log_sigmoid(a_pre) * (1.0 / GLA_TAU)
    la_hi = log_a.astype(BF16)
    la_lo = (log_a - la_hi.astype(F32)).astype(BF16)
    tri = tri_ref[...]
    cum_ref[...] = _dot(tri, la_hi) + _dot(tri, la_lo)
    gq_ref[...] = (proj(o, o + kw) * (GLA_DK ** -0.5)).astype(BF16); o += kw
    gk_ref[...] = proj(o, o + kw).astype(BF16); o += kw
    gv_ref[...] = proj(o, o + vw).astype(BF16); o += vw
    r = proj(o, o + vw)
    gr_ref[...] = (r * _sigmoid(r)).astype(BF16)

    d = sgs_ref.shape[-1]
    sgs_ref[...] = _sigmoid(_dot(xn, wgate_ref[:, :d]) + bgate_ref[0:1, :]).astype(BF16)
    sgg_ref[...] = _sigmoid(_dot(xn, wgate_ref[:, d:]) + bgate_ref[1:2, :]).astype(BF16)


def _in_proj(x2, norm_g, w_in, b_gate, w_alpha_up, b_alpha):
    t, d = x2.shape
    sb_w = SB_HEADS * SB_DH
    kw = GLA_HEADS * GLA_DK
    vw = GLA_HEADS * GLA_DV
    n_qkv = 3 * sb_w + 2 * kw + 2 * vw
    wqkv = w_in[:, :n_qkv].astype(BF16)
    wa = jnp.pad(w_in[:, n_qkv:n_qkv + GLA_RANK], ((0, 0), (0, LANES - GLA_RANK))).astype(BF16)
    wgate = w_in[:, n_qkv + GLA_RANK:].astype(BF16)
    wup = jnp.pad(w_alpha_up, ((0, LANES - GLA_RANK), (0, 0))).astype(BF16)
    row = lax.broadcasted_iota(jnp.int32, (TM, TM), 0)
    col = lax.broadcasted_iota(jnp.int32, (TM, TM), 1)
    tri = ((row // GLA_CHUNK == col // GLA_CHUNK) & (col <= row)).astype(BF16)

    tok = lambda w: pl.BlockSpec((TM, w), lambda i: (i, 0))
    out_shapes = [jax.ShapeDtypeStruct((t, w), BF16) for w in (sb_w, sb_w, sb_w, kw, kw, vw, vw)]
    out_shapes += [jax.ShapeDtypeStruct((t, kw), F32),
                   jax.ShapeDtypeStruct((t, d), BF16), jax.ShapeDtypeStruct((t, d), BF16)]
    out_specs = [tok(s.shape[1]) for s in out_shapes]
    return pl.pallas_call(
        _in_proj_kernel,
        grid=(t // TM,),
        in_specs=[tok(d), _const_spec((1, d)), _const_spec(wqkv.shape), _const_spec(wa.shape),
                  _const_spec(wup.shape), _const_spec((1, kw)), _const_spec(wgate.shape),
                  _const_spec((2, d)), _const_spec((TM, TM))],
        out_specs=out_specs,
        out_shape=out_shapes,
        compiler_params=pltpu.CompilerParams(dimension_semantics=("arbitrary",),
                                             vmem_limit_bytes=VMEM_LIMIT),
        name="in_proj",
    )(x2, norm_g.reshape(1, d), wqkv, wa, wup, b_alpha.reshape(1, kw), wgate, b_gate, tri)


SB_STAGES = 4
SB_UNROLL = 12


def _sb_kernel(qtab_ref, ktab_ref, atab_ref, q_ref, k_ref, v_ref, tri_ref, bias_ref, o_ref,
               qm_buf, vm_buf, z_buf, sp_buf, w_buf, carry_ref, acc_ref, *, n_tiles):
    lane = lax.broadcasted_iota(jnp.int32, (1, LANES), 1)
    zero = jnp.zeros((), BF16)
    for hh in range(2):
        hmask = (lane >= hh * SB_DH) & (lane < (hh + 1) * SB_DH)
        qm_buf[hh] = jnp.where(hmask, q_ref[0], zero)
        vm_buf[hh] = jnp.where(hmask, v_ref[0], zero)
    z_buf[2] = jnp.zeros(z_buf.shape[1:], F32)
    z_buf[3] = jnp.zeros(z_buf.shape[1:], F32)
    sp_buf[0] = jnp.zeros(sp_buf.shape[1:], BF16)
    w_buf[1] = jnp.zeros(w_buf.shape[1:], BF16)
    carry_ref[...] = jnp.zeros_like(carry_ref)
    acc_ref[...] = jnp.zeros_like(acc_ref)

    def rows(tile_idx, size):
        return pl.ds(pl.multiple_of(tile_idx * size, size), size)

    def step(i, u):
        a3 = atab_ref[i + 1]
        p3 = u & 1
        z3 = (u + 2) & 3
        neg_tri = tri_ref[...]
        for hh in range(2):
            x = _dot(sp_buf[p3, hh], neg_tri)
            carry = carry_ref[a3, hh]
            lw = z_buf[z3, hh] + x + jnp.concatenate([carry] * (SB_TK // LANES), axis=1)
            w_buf[p3, hh] = jnp.exp2(lw).astype(BF16)
            carry_ref[a3, hh] = carry + jnp.broadcast_to(x[:, 0:1], (SB_TQ, LANES))

        q4 = qtab_ref[i]
        k4 = ktab_ref[i]
        a4 = atab_ref[i]
        p4 = (u + 1) & 1
        pv = (_dot(w_buf[p4, 0], vm_buf[0, rows(k4, SB_TK), :])
              + _dot(w_buf[p4, 1], vm_buf[1, rows(k4, SB_TK), :]))
        acc = acc_ref[a4] + pv
        acc_ref[a4] = acc
        o_ref[0, rows(q4, SB_TQ), :] = acc.astype(BF16)

        p2 = (u + 1) & 1
        z2 = (u + 3) & 3
        for hh in range(2):
            z = z_buf[z2, hh]
            sp = jnp.maximum(z, 0.0) + jnp.log(1.0 + jnp.exp2(-jnp.abs(z))) * LOG2_E
            sp_buf[p2, hh] = sp.astype(BF16)

        q1 = qtab_ref[i + 3]
        k1 = ktab_ref[i + 3]
        z1 = u & 3
        kt = k_ref[0, rows(k1, SB_TK), :]
        bias = bias_ref[(k1 == q1).astype(jnp.int32)]
        for hh in range(2):
            z_buf[z1, hh] = _nt_dot(qm_buf[hh, rows(q1, SB_TQ), :], kt) + bias

    def body(trip, _):
        for u in range(SB_UNROLL):
            step(trip * SB_UNROLL + u, u)
        return 0

    n_iter = n_tiles + SB_STAGES - 1
    lax.fori_loop(0, n_iter // SB_UNROLL, body, 0)
    for i in range(n_iter // SB_UNROLL * SB_UNROLL, n_iter):
        step(i, i % SB_UNROLL)


def _sb_attn(q, k, v):
    b, s, w = q.shape
    nq = s // SB_TQ
    fill = [(0, 0, nq)] * (SB_STAGES - 1)
    tiles = fill + [(qi, kj, qi) for qi in range(nq) for kj in range(qi, -1, -1)] + fill
    n_tiles = len(tiles) - 2 * len(fill)
    qtab, ktab, atab = (jnp.asarray([t[c] for t in tiles], jnp.int32) for c in range(3))
    row = lax.broadcasted_iota(jnp.int32, (SB_TK, SB_TK), 0)
    col = lax.broadcasted_iota(jnp.int32, (SB_TK, SB_TK), 1)
    tri = -(row >= col).astype(BF16)
    bias = jnp.stack([jnp.zeros((SB_TQ, SB_TK), F32),
                      jnp.where(col < row, 0.0, -1e30).astype(F32)])
    seq_spec = pl.BlockSpec((1, s, LANES), lambda bi, p, *_: (bi, 0, p))
    grid_spec = pltpu.PrefetchScalarGridSpec(
        num_scalar_prefetch=3,
        grid=(b, w // LANES),
        in_specs=[seq_spec, seq_spec, seq_spec,
                  pl.BlockSpec((SB_TK, SB_TK), lambda bi, p, *_: (0, 0)),
                  pl.BlockSpec((2, SB_TQ, SB_TK), lambda bi, p, *_: (0, 0, 0))],
        out_specs=seq_spec,
        scratch_shapes=[pltpu.VMEM((2, s, LANES), BF16),
                        pltpu.VMEM((2, s, LANES), BF16),
                        pltpu.VMEM((4, 2, SB_TQ, SB_TK), F32),
                        pltpu.VMEM((2, 2, SB_TQ, SB_TK), BF16),
                        pltpu.VMEM((2, 2, SB_TQ, SB_TK), BF16),
                        pltpu.VMEM((nq + 1, 2, SB_TQ, LANES), F32),
                        pltpu.VMEM((nq + 1, SB_TQ, LANES), F32)],
    )
    return pl.pallas_call(
        functools.partial(_sb_kernel, n_tiles=n_tiles),
        grid_spec=grid_spec,
        out_shape=jax.ShapeDtypeStruct((b, s, w), BF16),
        compiler_params=pltpu.CompilerParams(dimension_semantics=("arbitrary", "arbitrary"),
                                             vmem_limit_bytes=VMEM_LIMIT),
        name="sb_attn",
    )(qtab, ktab, atab, q, k, v, tri, bias)


def _gla_kernel(q_ref, k_ref, v_ref, cum_ref, r_ref, g_ref, o_ref, st_ref):
    @pl.when(pl.program_id(1) == 0)
    def _():
        st_ref[...] = jnp.zeros_like(st_ref)

    c = GLA_CHUNK
    lane = lax.broadcasted_iota(jnp.int32, (1, LANES), 1)
    row = lax.broadcasted_iota(jnp.int32, (c, c), 0)
    col = lax.broadcasted_iota(jnp.int32, (c, c), 1)
    causal = col <= row
    for n in range(GLA_BLK // c):
        rows = slice(n * c, (n + 1) * c)
        for p in range(GLA_HEADS // 2):
            lanes = slice(p * LANES, (p + 1) * LANES)
            cum = cum_ref[0, rows, lanes]
            cum_last = cum[c - 1:c, :]
            qf = q_ref[0, rows, lanes].astype(F32)
            kf = k_ref[0, rows, lanes].astype(F32)
            q_dec = qf * jnp.exp(cum)
            k_inv = (kf * jnp.exp(-cum)).astype(BF16)
            k_end = kf * jnp.exp(cum_last - cum)
            decay = jnp.exp(cum_last)
            for hh in range(2):
                h = 2 * p + hh
                hmask = (lane >= hh * GLA_DK) & (lane < (hh + 1) * GLA_DK)
                vlanes = slice(h * GLA_DV, (h + 1) * GLA_DV)
                vb = v_ref[0, rows, vlanes]
                qd = jnp.where(hmask, q_dec, 0.0).astype(BF16)
                ke = jnp.where(hmask, k_end, 0.0).astype(BF16)
                att = jnp.where(causal, _nt_dot(qd, k_inv), 0.0)
                st = st_ref[h]
                o = _dot(att.astype(BF16), vb) + _nt_dot(qd, st.astype(BF16))
                vt = vb.astype(F32).T.astype(BF16)
                st_ref[h] = st * decay + _dot(vt, ke)
                o = o * lax.rsqrt(jnp.mean(o * o, axis=-1, keepdims=True) + EPS)
                o = o * g_ref[0:1, vlanes] * r_ref[0, rows, vlanes].astype(F32)
                o_ref[0, rows, vlanes] = o.astype(BF16)


def _gla(gq, gk, gv, cum, r_act, norm_g):
    b, s, kw = gq.shape
    vw = gv.shape[-1]
    spec = lambda w: pl.BlockSpec((1, GLA_BLK, w), lambda bi, j: (bi, j, 0))
    return pl.pallas_call(
        _gla_kernel,
        grid=(b, s // GLA_BLK),
        in_specs=[spec(kw), spec(kw), spec(vw), spec(kw), spec(vw), _const_spec((1, vw))],
        out_specs=spec(vw),
        out_shape=jax.ShapeDtypeStruct((b, s, vw), BF16),
        scratch_shapes=[pltpu.VMEM((GLA_HEADS, GLA_DV, LANES), F32)],
        compiler_params=pltpu.CompilerParams(dimension_semantics=("arbitrary", "arbitrary"),
                                             vmem_limit_bytes=VMEM_LIMIT),
        name="gla",
    )(gq, gk, gv, cum, r_act, norm_g.reshape(1, vw))


def _mix_out_kernel(x_ref, osb_ref, ogla_ref, sgs_ref, sgg_ref, wsb_ref, wgla_ref, wout_ref, h_ref):
    y = (sgs_ref[...].astype(F32) * _dot(osb_ref[...], wsb_ref[...])
         + sgg_ref[...].astype(F32) * _dot(ogla_ref[...], wgla_ref[...]))
    h_ref[...] = x_ref[...] + _dot(y.astype(BF16), wout_ref[...])


def _mix_out(x2, o_sb, o_gla, sg_sb, sg_gla, w_branch_sb, w_branch_gla, w_out):
    t, d = x2.shape
    tok = lambda w: pl.BlockSpec((TM, w), lambda i: (i, 0))
    wsb, wgla, wout = (w.astype(BF16) for w in (w_branch_sb, w_branch_gla, w_out))
    return pl.pallas_call(
        _mix_out_kernel,
        grid=(t // TM,),
        in_specs=[tok(d), tok(o_sb.shape[1]), tok(o_gla.shape[1]), tok(d), tok(d),
                  _const_spec(wsb.shape), _const_spec(wgla.shape), _const_spec(wout.shape)],
        out_specs=tok(d),
        out_shape=jax.ShapeDtypeStruct((t, d), F32),
        compiler_params=pltpu.CompilerParams(dimension_semantics=("arbitrary",),
                                             vmem_limit_bytes=VMEM_LIMIT),
        name="mix_out",
    )(x2, o_sb, o_gla, sg_sb, sg_gla, wsb, wgla, wout)


GELU_C0 = 0.7978845608028654
GELU_C1 = 0.7978845608028654 * 0.044715


def _conv_ffn_kernel(h_ref, g_ref, wag_ref, cw_ref, cb_ref, wout_ref, gf_ref, o_ref,
                     hn_ref, act_ref, acc_ref, prev_ref, *, tiles_per_seq, final_norm):
    n_ck = wag_ref.shape[0]
    tm = h_ref.shape[0]

    @pl.when(pl.program_id(0) % tiles_per_seq == 0)
    def _():
        prev_ref[...] = jnp.zeros_like(prev_ref)

    h = h_ref[...]
    ms = jnp.mean(h * h, axis=-1, keepdims=True)
    hn_ref[...] = (h * lax.rsqrt(ms + EPS) * g_ref[...]).astype(BF16)

    row_blocks = [slice(r, r + FF_RB) for r in range(0, tm, FF_RB)]

    def out_proj(c):
        for rb in row_blocks:
            part = _dot(act_ref[c % 2, rb, :], wout_ref[c])
            if c == 0:
                acc_ref[rb, :] = part
            else:
                acc_ref[rb, :] += part

    for c in range(n_ck):
        hn = hn_ref[...]
        ag = _dot(hn, wag_ref[c])
        a = ag[:, :FF_CK]
        gate = ag[:, FF_CK:]
        if c > 0:
            out_proj(c - 1)
        ext = jnp.concatenate([prev_ref[c], a], axis=0)
        prev_ref[c] = a[tm - SUBLANES:, :]
        a1 = pltpu.roll(ext, 1, 0)[SUBLANES:, :]
        a2 = pltpu.roll(ext, 2, 0)[SUBLANES:, :]
        cw = cw_ref[c]
        for rb in row_blocks:
            conv = cb_ref[c] + a2[rb] * cw[0:1, :] + a1[rb] * cw[1:2, :] + a[rb] * cw[2:3, :]
            poly = (GELU_C0 + GELU_C1 * (conv * conv)).astype(BF16)
            xb = conv.astype(BF16)
            gb = gate[rb].astype(BF16)
            act_ref[c % 2, rb, :] = (xb * gb) * (1.0 + jnp.tanh(xb * poly))
    out_proj(n_ck - 1)
    h2 = h + acc_ref[...]
    if final_norm:
        ms2 = jnp.mean(h2 * h2, axis=-1, keepdims=True)
        h2 = h2 * lax.rsqrt(ms2 + EPS) * gf_ref[...]
    o_ref[...] = h2


def _conv_ffn(h2d, seq_len, norm_g, w_ffn_in, conv_w, conv_b, w_ffn_out, norm_final_g, final_norm):
    t, d = h2d.shape
    d_ff = w_ffn_out.shape[0]
    n_ck = d_ff // FF_CK
    tok = pl.BlockSpec((TM, d), lambda i: (i, 0))
    chunked = lambda w: w.reshape(w.shape[0], n_ck, FF_CK).transpose(1, 0, 2)
    wag = jnp.concatenate([chunked(w_ffn_in[:, :d_ff]), chunked(w_ffn_in[:, d_ff:])],
                          axis=-1).astype(BF16)
    wout = (0.5 * w_ffn_out).astype(BF16).reshape(n_ck, FF_CK, d)
    cw = chunked(conv_w)
    cb = chunked(conv_b.reshape(1, d_ff))
    kern = functools.partial(_conv_ffn_kernel, tiles_per_seq=seq_len // TM, final_norm=final_norm)
    return pl.pallas_call(
        kern,
        grid=(t // TM,),
        in_specs=[tok, _const_spec((1, d)), _const_spec(wag.shape), _const_spec(cw.shape), _const_spec(cb.shape), _const_spec(wout.shape),
                  _const_spec((1, d))],
        out_specs=tok,
        out_shape=jax.ShapeDtypeStruct((t, d), F32),
        scratch_shapes=[pltpu.VMEM((TM, d), BF16),
                        pltpu.VMEM((2, TM, FF_CK), BF16),
                        pltpu.VMEM((TM, d), F32),
                        pltpu.VMEM((n_ck, SUBLANES, FF_CK), F32)],
        compiler_params=pltpu.CompilerParams(dimension_semantics=("arbitrary",),
                                             vmem_limit_bytes=VMEM_LIMIT),
        name="conv_ffn",
    )(h2d, norm_g.reshape(1, d), wag, cw, cb, wout, norm_final_g.reshape(1, d))


def kernel(x, norm_mix_g, w_in, b_gate, w_alpha_up, b_alpha, gla_norm_g, w_branch_sb, w_branch_gla,
           w_out, norm_ffn_g, w_ffn_in, conv_w, conv_b, w_ffn_out, norm_final_g):
    b, s, d = x.shape
    depth = w_in.shape[0]
    assert s % TM == 0 and s % SB_TQ == 0 and s % GLA_BLK == 0 and (b * s) % TM == 0
    assert w_ffn_out.shape[1] % FF_CK == 0 and conv_w.shape[1] == CONV_WIDTH
    assert SB_UNROLL % 4 == 0 and TM % FF_RB == 0
    h = x.reshape(b * s, d)
    for i in range(depth):
        (sbq, sbk, sbv, gq, gk, gv, r_act, cum, sg_sb, sg_gla) = _in_proj(
            h, norm_mix_g[i], w_in[i], b_gate[i], w_alpha_up[i], b_alpha[i])
        seq = lambda a: a.reshape(b, s, a.shape[-1])
        o_sb = _sb_attn(seq(sbq), seq(sbk), seq(sbv)).reshape(b * s, -1)
        o_gla = _gla(seq(gq), seq(gk), seq(gv), seq(cum), seq(r_act), gla_norm_g[i])
        h = _mix_out(h, o_sb, o_gla.reshape(b * s, -1), sg_sb, sg_gla,
                     w_branch_sb[i], w_branch_gla[i], w_out[i])
        h = _conv_ffn(h, s, norm_ffn_g[i], w_ffn_in[i], conv_w[i], conv_b[i], w_ffn_out[i],
                      norm_final_g, final_norm=(i == depth - 1))
    return h.reshape(b, s, d)
```

```python
import functools

import jax
import jax.numpy as jnp
from jax import lax
from jax.experimental import pallas as pl
from jax.experimental.pallas import tpu as pltpu

F32 = jnp.float32
BF16 = jnp.bfloat16

EPS = 1e-6
LOG2_E = 1.4426950408889634
SB_HEADS = 8
SB_DH = 64
GLA_HEADS = 4
GLA_DK = 64
GLA_DV = 128
GLA_RANK = 16
GLA_TAU = 16.0
GLA_CHUNK = 64
CONV_WIDTH = 3

LANES = 128
SUBLANES = 8
VMEM_LIMIT = 56 * 1024 * 1024

TM = 512
FF_TM = 256
SB_TQ = 256
SB_TK = 256
GLA_BLK = 512
FF_CK = 256
FF_RB = 256


def _nt_dot(a, b):
    return lax.dot_general(a, b, (((1,), (1,)), ((), ())), preferred_element_type=F32)


def _dot(a, b):
    return jnp.dot(a, b, preferred_element_type=F32)


def _log_sigmoid(z):
    return jnp.minimum(z, 0.0) - jnp.log(1.0 + jnp.exp(-jnp.abs(z)))


def _sigmoid(z):
    return 1.0 / (1.0 + jnp.exp(-z))


def _const_spec(shape):
    return pl.BlockSpec(shape, lambda *_: (0,) * len(shape))


def _in_proj_kernel(x_ref, g_ref, wqkv_ref, wa_ref, wup_ref, balpha_ref, wgate_ref, bgate_ref,
                    tri_ref, sbq_ref, sbk_ref, sbv_ref, gq_ref, gk_ref, gv_ref, gr_ref, cum_ref,
                    sgs_ref, sgg_ref):
    x = x_ref[...]
    ms = jnp.mean(x * x, axis=-1, keepdims=True)
    xn = (x * lax.rsqrt(ms + EPS) * g_ref[...]).astype(BF16)

    def proj(lo, hi):
        return _dot(xn, wqkv_ref[:, lo:hi])

    sb_w = SB_HEADS * SB_DH
    kw = GLA_HEADS * GLA_DK
    vw = GLA_HEADS * GLA_DV
    code = _dot(xn, wa_ref[...])
    o = 0
    sbq_ref[...] = (proj(o, o + sb_w) * (SB_DH ** -0.5 * LOG2_E)).astype(BF16); o += sb_w
    a_pre = _dot(code.astype(BF16), wup_ref[...]) + balpha_ref[...]
    sbk_ref[...] = proj(o, o + sb_w).astype(BF16); o += sb_w
    sbv_ref[...] = proj(o, o + sb_w).astype(BF16); o += sb_w
    log_a = _log_sigmoid(a_pre) * (1.0 / GLA_TAU)
    la_hi = log_a.astype(BF16)
    la_lo = (log_a - la_hi.astype(F32)).astype(BF16)
    tri = tri_ref[...]
    cum_ref[...] = _dot(tri, la_hi) + _dot(tri, la_lo)
    gq_ref[...] = (proj(o, o + kw) * (GLA_DK ** -0.5)).astype(BF16); o += kw
    gk_ref[...] = proj(o, o + kw).astype(BF16); o += kw
    gv_ref[...] = proj(o, o + vw).astype(BF16); o += vw
    r = proj(o, o + vw)
    gr_ref[...] = (r * _sigmoid(r)).astype(BF16)

    d = sgs_ref.shape[-1]
    sgs_ref[...] = _sigmoid(_dot(xn, wgate_ref[:, :d]) + bgate_ref[0:1, :]).astype(BF16)
    sgg_ref[...] = _sigmoid(_dot(xn, wgate_ref[:, d:]) + bgate_ref[1:2, :]).astype(BF16)


def _in_proj(x2, norm_g, w_in, b_gate, w_alpha_up, b_alpha):
    t, d = x2.shape
    sb_w = SB_HEADS * SB_DH
    kw = GLA_HEADS * GLA_DK
    vw = GLA_HEADS * GLA_DV
    n_qkv = 3 * sb_w + 2 * kw + 2 * vw
    wqkv = w_in[:, :n_qkv].astype(BF16)
    wa = jnp.pad(w_in[:, n_qkv:n_qkv + GLA_RANK], ((0, 0), (0, LANES - GLA_RANK))).astype(BF16)
    wgate = w_in[:, n_qkv + GLA_RANK:].astype(BF16)
    wup = jnp.pad(w_alpha_up, ((0, LANES - GLA_RANK), (0, 0))).astype(BF16)
    row = lax.broadcasted_iota(jnp.int32, (TM, TM), 0)
    col = lax.broadcasted_iota(jnp.int32, (TM, TM), 1)
    tri = ((row // GLA_CHUNK == col // GLA_CHUNK) & (col <= row)).astype(BF16)

    tok = lambda w: pl.BlockSpec((TM, w), lambda i: (i, 0))
    out_shapes = [jax.ShapeDtypeStruct((t, w), BF16) for w in (sb_w, sb_w, sb_w, kw, kw, vw, vw)]
    out_shapes += [jax.ShapeDtypeStruct((t, kw), F32),
                   jax.ShapeDtypeStruct((t, d), BF16), jax.ShapeDtypeStruct((t, d), BF16)]
    out_specs = [tok(s.shape[1]) for s in out_shapes]
    return pl.pallas_call(
        _in_proj_kernel,
        grid=(t // TM,),
        in_specs=[tok(d), _const_spec((1, d)), _const_spec(wqkv.shape), _const_spec(wa.shape),
                  _const_spec(wup.shape), _const_spec((1, kw)), _const_spec(wgate.shape),
                  _const_spec((2, d)), _const_spec((TM, TM))],
        out_specs=out_specs,
        out_shape=out_shapes,
        compiler_params=pltpu.CompilerParams(dimension_semantics=("arbitrary",),
                                             vmem_limit_bytes=VMEM_LIMIT),
        name="in_proj",
    )(x2, norm_g.reshape(1, d), wqkv, wa, wup, b_alpha.reshape(1, kw), wgate, b_gate, tri)


SB_STAGES = 4
SB_UNROLL = 12


def _sb_kernel(qtab_ref, ktab_ref, atab_ref, q_ref, k_ref, v_ref, tri_ref, bias_ref, o_ref,
               qm_buf, vm_buf, z_buf, sp_buf, w_buf, carry_ref, acc_ref, *, n_tiles):
    lane = lax.broadcasted_iota(jnp.int32, (1, LANES), 1)
    zero = jnp.zeros((), BF16)
    for hh in range(2):
        hmask = (lane >= hh * SB_DH) & (lane < (hh + 1) * SB_DH)
        qm_buf[hh] = jnp.where(hmask, q_ref[0], zero)
        vm_buf[hh] = jnp.where(hmask, v_ref[0], zero)
    z_buf[2] = jnp.zeros(z_buf.shape[1:], F32)
    z_buf[3] = jnp.zeros(z_buf.shape[1:], F32)
    sp_buf[0] = jnp.zeros(sp_buf.shape[1:], BF16)
    w_buf[1] = jnp.zeros(w_buf.shape[1:], BF16)
    carry_ref[...] = jnp.zeros_like(carry_ref)
    acc_ref[...] = jnp.zeros_like(acc_ref)

    def rows(tile_idx, size):
        return pl.ds(pl.multiple_of(tile_idx * size, size), size)

    def step(i, u):
        a3 = atab_ref[i + 1]
        p3 = u & 1
        z3 = (u + 2) & 3
        neg_tri = tri_ref[...]
        for hh in range(2):
            x = _dot(sp_buf[p3, hh], neg_tri)
            carry = carry_ref[a3, hh]
            lw = z_buf[z3, hh] + x + jnp.concatenate([carry] * (SB_TK // LANES), axis=1)
            w_buf[p3, hh] = jnp.exp2(lw).astype(BF16)
            carry_ref[a3, hh] = carry + jnp.broadcast_to(x[:, 0:1], (SB_TQ, LANES))

        q4 = qtab_ref[i]
        k4 = ktab_ref[i]
        a4 = atab_ref[i]
        p4 = (u + 1) & 1
        pv = (_dot(w_buf[p4, 0], vm_buf[0, rows(k4, SB_TK), :])
              + _dot(w_buf[p4, 1], vm_buf[1, rows(k4, SB_TK), :]))
        acc = acc_ref[a4] + pv
        acc_ref[a4] = acc
        o_ref[0, rows(q4, SB_TQ), :] = acc.astype(BF16)

        p2 = (u + 1) & 1
        z2 = (u + 3) & 3
        for hh in range(2):
            z = z_buf[z2, hh]
            sp = jnp.maximum(z, 0.0) + jnp.log(1.0 + jnp.exp2(-jnp.abs(z))) * LOG2_E
            sp_buf[p2, hh] = sp.astype(BF16)

        q1 = qtab_ref[i + 3]
        k1 = ktab_ref[i + 3]
        z1 = u & 3
        kt = k_ref[0, rows(k1, SB_TK), :]
        bias = bias_ref[(k1 == q1).astype(jnp.int32)]
        for hh in range(2):
            z_buf[z1, hh] = _nt_dot(qm_buf[hh, rows(q1, SB_TQ), :], kt) + bias

    def body(trip, _):
        for u in range(SB_UNROLL):
            step(trip * SB_UNROLL + u, u)
        return 0

    n_iter = n_tiles + SB_STAGES - 1
    lax.fori_loop(0, n_iter // SB_UNROLL, body, 0)
    for i in range(n_iter // SB_UNROLL * SB_UNROLL, n_iter):
        step(i, i % SB_UNROLL)


def _sb_attn(q, k, v):
    b, s, w = q.shape
    nq = s // SB_TQ
    fill = [(0, 0, nq)] * (SB_STAGES - 1)
    tiles = fill + [(qi, kj, qi) for qi in range(nq) for kj in range(qi, -1, -1)] + fill
    n_tiles = len(tiles) - 2 * len(fill)
    qtab, ktab, atab = (jnp.asarray([t[c] for t in tiles], jnp.int32) for c in range(3))
    row = lax.broadcasted_iota(jnp.int32, (SB_TK, SB_TK), 0)
    col = lax.broadcasted_iota(jnp.int32, (SB_TK, SB_TK), 1)
    tri = -(row >= col).astype(BF16)
    bias = jnp.stack([jnp.zeros((SB_TQ, SB_TK), F32),
                      jnp.where(col < row, 0.0, -1e30).astype(F32)])
    seq_spec = pl.BlockSpec((1, s, LANES), lambda bi, p, *_: (bi, 0, p))
    grid_spec = pltpu.PrefetchScalarGridSpec(
        num_scalar_prefetch=3,
        grid=(b, w // LANES),
        in_specs=[seq_spec, seq_spec, seq_spec,
                  pl.BlockSpec((SB_TK, SB_TK), lambda bi, p, *_: (0, 0)),
                  pl.BlockSpec((2, SB_TQ, SB_TK), lambda bi, p, *_: (0, 0, 0))],
        out_specs=seq_spec,
        scratch_shapes=[pltpu.VMEM((2, s, LANES), BF16),
                        pltpu.VMEM((2, s, LANES), BF16),
                        pltpu.VMEM((4, 2, SB_TQ, SB_TK), F32),
                        pltpu.VMEM((2, 2, SB_TQ, SB_TK), BF16),
                        pltpu.VMEM((2, 2, SB_TQ, SB_TK), BF16),
                        pltpu.VMEM((nq + 1, 2, SB_TQ, LANES), F32),
                        pltpu.VMEM((nq + 1, SB_TQ, LANES), F32)],
    )
    return pl.pallas_call(
        functools.partial(_sb_kernel, n_tiles=n_tiles),
        grid_spec=grid_spec,
        out_shape=jax.ShapeDtypeStruct((b, s, w), BF16),
        compiler_params=pltpu.CompilerParams(dimension_semantics=("arbitrary", "arbitrary"),
                                             vmem_limit_bytes=VMEM_LIMIT),
        name="sb_attn",
    )(qtab, ktab, atab, q, k, v, tri, bias)


def _gla_block(q_ref, k_ref, v_ref, cum_ref, r_ref, g_ref, o_ref, st_ref):
    c = GLA_CHUNK
    lane = lax.broadcasted_iota(jnp.int32, (1, LANES), 1)
    row = lax.broadcasted_iota(jnp.int32, (c, c), 0)
    col = lax.broadcasted_iota(jnp.int32, (c, c), 1)
    causal = col <= row
    for n in range(GLA_BLK // c):
        rows = slice(n * c, (n + 1) * c)
        for p in range(GLA_HEADS // 2):
            lanes = slice(p * LANES, (p + 1) * LANES)
            cum = cum_ref[rows, lanes]
            cum_last = cum[c - 1:c, :]
            qf = q_ref[rows, lanes].astype(F32)
            kf = k_ref[rows, lanes].astype(F32)
            q_dec = qf * jnp.exp(cum)
            k_inv = (kf * jnp.exp(-cum)).astype(BF16)
            k_end = kf * jnp.exp(cum_last - cum)
            decay = jnp.exp(cum_last)
            for hh in range(2):
                h = 2 * p + hh
                hmask = (lane >= hh * GLA_DK) & (lane < (hh + 1) * GLA_DK)
                vlanes = slice(h * GLA_DV, (h + 1) * GLA_DV)
                vb = v_ref[rows, vlanes]
                qd = jnp.where(hmask, q_dec, 0.0).astype(BF16)
                ke = jnp.where(hmask, k_end, 0.0).astype(BF16)
                att = jnp.where(causal, _nt_dot(qd, k_inv), 0.0)
                st = st_ref[h]
                o = _dot(att.astype(BF16), vb) + _nt_dot(qd, st.astype(BF16))
                vt = vb.astype(F32).T.astype(BF16)
                st_ref[h] = st * decay + _dot(vt, ke)
                o = o * lax.rsqrt(jnp.mean(o * o, axis=-1, keepdims=True) + EPS)
                o = o * g_ref[0:1, vlanes] * r_ref[rows, vlanes].astype(F32)
                o_ref[rows, vlanes] = o.astype(BF16)


def _gla_mix_kernel(x_ref, osb_ref, gq_ref, gk_ref, gv_ref, cum_ref, r_ref, g_ref, sgs_ref, sgg_ref,
                    wsb_ref, wgla_ref, wout_ref, h_ref, st_ref, ogla_ref, y_ref, *, tiles_per_seq):
    @pl.when(pl.program_id(0) % tiles_per_seq == 0)
    def _():
        st_ref[...] = jnp.zeros_like(st_ref)

    y_ref[...] = sgs_ref[...].astype(F32) * _dot(osb_ref[...], wsb_ref[...])
    _gla_block(gq_ref, gk_ref, gv_ref, cum_ref, r_ref, g_ref, ogla_ref, st_ref)
    y = y_ref[...] + sgg_ref[...].astype(F32) * _dot(ogla_ref[...], wgla_ref[...])
    h_ref[...] = x_ref[...] + _dot(y.astype(BF16), wout_ref[...])


def _gla_mix(x2, seq_len, o_sb, gq, gk, gv, cum, r_act, norm_g, sg_sb, sg_gla,
             w_branch_sb, w_branch_gla, w_out):
    t, d = x2.shape
    vw = gv.shape[1]
    tok = lambda a: pl.BlockSpec((GLA_BLK, a.shape[1]), lambda i: (i, 0))
    wsb, wgla, wout = (w.astype(BF16) for w in (w_branch_sb, w_branch_gla, w_out))
    tokens = (x2, o_sb, gq, gk, gv, cum, r_act)
    gates = (sg_sb, sg_gla)
    consts = (wsb, wgla, wout)
    return pl.pallas_call(
        functools.partial(_gla_mix_kernel, tiles_per_seq=seq_len // GLA_BLK),
        grid=(t // GLA_BLK,),
        in_specs=([tok(a) for a in tokens] + [_const_spec((1, vw))] + [tok(a) for a in gates]
                  + [_const_spec(w.shape) for w in consts]),
        out_specs=tok(x2),
        out_shape=jax.ShapeDtypeStruct((t, d), F32),
        scratch_shapes=[pltpu.VMEM((GLA_HEADS, GLA_DV, LANES), F32),
                        pltpu.VMEM((GLA_BLK, vw), BF16),
                        pltpu.VMEM((GLA_BLK, d), F32)],
        compiler_params=pltpu.CompilerParams(dimension_semantics=("arbitrary",),
                                             vmem_limit_bytes=VMEM_LIMIT),
        name="gla_mix",
    )(*tokens, norm_g.reshape(1, vw), *gates, *consts)


GELU_C0 = 0.7978845608028654
GELU_K = 0.044715 ** 0.5 / GELU_C0


def _conv_ffn_kernel(h_ref, g_ref, wag_ref, cw_ref, cb_ref, wout_ref, gf_ref, o_ref,
                     hn_ref, act_ref, acc_ref, prev_ref, *, tiles_per_seq, final_norm):
    n_ck = wag_ref.shape[0]
    tm = h_ref.shape[0]

    @pl.when(pl.program_id(0) % tiles_per_seq == 0)
    def _():
        prev_ref[...] = jnp.zeros_like(prev_ref)

    h = h_ref[...]
    ms = jnp.mean(h * h, axis=-1, keepdims=True)
    hn_ref[...] = (h * lax.rsqrt(ms + EPS) * g_ref[...]).astype(BF16)

    row_blocks = [slice(r, r + FF_RB) for r in range(0, tm, FF_RB)]

    def out_proj(c):
        for rb in row_blocks:
            part = _dot(act_ref[c % 2, rb, :], wout_ref[c])
            if c == 0:
                acc_ref[rb, :] = part
            else:
                acc_ref[rb, :] += part

    for c in range(n_ck):
        hn = hn_ref[...]
        ag = _dot(hn, wag_ref[c])
        a = ag[:, :FF_CK]
        gate = ag[:, FF_CK:]
        if c > 0:
            out_proj(c - 1)
        ext = jnp.concatenate([prev_ref[c], a], axis=0)
        prev_ref[c] = a[tm - SUBLANES:, :]
        a1 = pltpu.roll(ext, 1, 0)[SUBLANES:, :]
        a2 = pltpu.roll(ext, 2, 0)[SUBLANES:, :]
        cw = cw_ref[c]
        for rb in row_blocks:
            y = cb_ref[c] + a2[rb] * cw[0:1, :] + a1[rb] * cw[1:2, :] + a[rb] * cw[2:3, :]
            yb = y.astype(BF16)
            sb = (y * GELU_K).astype(BF16)
            gb = gate[rb].astype(BF16)
            act_ref[c % 2, rb, :] = (yb * gb) * (1.0 + jnp.tanh(yb * (1.0 + sb * sb)))
    out_proj(n_ck - 1)
    h2 = h + acc_ref[...]
    if final_norm:
        ms2 = jnp.mean(h2 * h2, axis=-1, keepdims=True)
        h2 = h2 * lax.rsqrt(ms2 + EPS) * gf_ref[...]
    o_ref[...] = h2


def _conv_ffn(h2d, seq_len, norm_g, w_ffn_in, conv_w, conv_b, w_ffn_out, norm_final_g, final_norm):
    t, d = h2d.shape
    d_ff = w_ffn_out.shape[0]
    n_ck = d_ff // FF_CK
    tok = pl.BlockSpec((FF_TM, d), lambda i: (i, 0))
    chunked =lambda w: w.reshape(w.shape[0], n_ck, FF_CK).transpose(1, 0, 2)
    wag = jnp.concatenate([chunked(w_ffn_in[:, :d_ff]), chunked(w_ffn_in[:, d_ff:])],
                          axis=-1).astype(BF16)
    wout = ((0.5 / GELU_C0) * w_ffn_out).astype(BF16).reshape(n_ck, FF_CK, d)
    cw = chunked(GELU_C0 * conv_w)
    cb = chunked(GELU_C0 * conv_b.reshape(1, d_ff))
    kern = functools.partial(_conv_ffn_kernel, tiles_per_seq=seq_len // FF_TM,
                             final_norm=final_norm)
    return pl.pallas_call(
        kern,
        grid=(t // FF_TM,),
        in_specs=[tok, _const_spec((1, d)), _const_spec(wag.shape), _const_spec(cw.shape),
                  _const_spec(cb.shape), _const_spec(wout.shape), _const_spec((1, d))],
        out_specs=tok,
        out_shape=jax.ShapeDtypeStruct((t, d), F32),
        scratch_shapes=[pltpu.VMEM((FF_TM, d), BF16),
                        pltpu.VMEM((2, FF_TM, FF_CK), BF16),
                        pltpu.VMEM((FF_TM, d), F32),
                        pltpu.VMEM((n_ck, SUBLANES, FF_CK), F32)],
        compiler_params=pltpu.CompilerParams(dimension_semantics=("arbitrary",),
                                             vmem_limit_bytes=VMEM_LIMIT),
        name="conv_ffn",
    )(h2d, norm_g.reshape(1, d), wag, cw, cb, wout, norm_final_g.reshape(1, d))


def kernel(x, norm_mix_g, w_in, b_gate, w_alpha_up, b_alpha, gla_norm_g, w_branch_sb, w_branch_gla,
           w_out, norm_ffn_g, w_ffn_in, conv_w, conv_b, w_ffn_out, norm_final_g):
    b, s, d = x.shape
    depth = w_in.shape[0]
    assert s % TM == 0 and s % SB_TQ == 0 and s % GLA_BLK == 0 and (b * s) % TM == 0
    assert w_ffn_out.shape[1] % FF_CK == 0 and conv_w.shape[1] == CONV_WIDTH
    assert SB_UNROLL % 4 == 0 and FF_TM % FF_RB == 0 and s % FF_TM == 0
    h = x.reshape(b * s, d)
    for i in range(depth):
        (sbq, sbk, sbv, gq, gk, gv, r_act, cum, sg_sb, sg_gla) = _in_proj(
            h, norm_mix_g[i], w_in[i], b_gate[i], w_alpha_up[i], b_alpha[i])
        seq = lambda a: a.reshape(b, s, a.shape[-1])
        o_sb = _sb_attn(seq(sbq), seq(sbk), seq(sbv)).reshape(b * s, -1)
        h = _gla_mix(h, s, o_sb, gq, gk, gv, cum, r_act, gla_norm_g[i], sg_sb, sg_gla,
                     w_branch_sb[i], w_branch_gla[i], w_out[i])
        h = _conv_ffn(h, s, norm_ffn_g[i], w_ffn_in[i], conv_w[i], conv_b[i], w_ffn_out[i],
                      norm_final_g, final_norm=(i == depth - 1))
    return h.reshape(b, s, d)
```

```python
import functools

import jax
import jax.numpy as jnp
from jax import lax
from jax.experimental import pallas as pl
from jax.experimental.pallas import tpu as pltpu

F32 = jnp.float32
BF16 = jnp.bfloat16

EPS = 1e-6
LOG2_E = 1.4426950408889634
SB_HEADS = 8
SB_DH = 64
GLA_HEADS = 4
GLA_DK = 64
GLA_DV = 128
GLA_RANK = 16
GLA_TAU = 16.0
GLA_CHUNK = 64
CONV_WIDTH = 3

LANES = 128
SUBLANES = 8
VMEM_LIMIT = 56 * 1024 * 1024

TM = 512
FF_TM = 256
SB_TQ = 256
SB_TK = 256
GLA_BLK = 512
FF_CK = 256
FF_RB = 256


def _nt_dot(a, b):
    return lax.dot_general(a, b, (((1,), (1,)), ((), ())), preferred_element_type=F32)


def _dot(a, b):
    return jnp.dot(a, b, preferred_element_type=F32)


def _log_sigmoid(z):
    return jnp.minimum(z, 0.0) - jnp.log(1.0 + jnp.exp(-jnp.abs(z)))


def _sigmoid(z):
    return 1.0 / (1.0 + jnp.exp(-z))


def _const_spec(shape):
    return pl.BlockSpec(shape, lambda *_: (0,) * len(shape))


def _in_proj_kernel(x_ref, g_ref, wqkv_ref, wa_ref, wup_ref, balpha_ref, wgate_ref, bgate_ref,
                    tri_ref, sbq_ref, sbk_ref, sbv_ref, gq_ref, gk_ref, gv_ref, gr_ref, cum_ref,
                    sgs_ref, sgg_ref):
    x = x_ref[...]
    ms = jnp.mean(x * x, axis=-1, keepdims=True)
    xn = (x * lax.rsqrt(ms + EPS) * g_ref[...]).astype(BF16)

    def proj(lo, hi):
        return _dot(xn, wqkv_ref[:, lo:hi])

    sb_w = SB_HEADS * SB_DH
    kw = GLA_HEADS * GLA_DK
    vw = GLA_HEADS * GLA_DV
    code = _dot(xn, wa_ref[...])
    o = 0
    sbq_ref[...] = (proj(o, o + sb_w) * (SB_DH ** -0.5 * LOG2_E)).astype(BF16); o += sb_w
    a_pre = _dot(code.astype(BF16), wup_ref[...]) + balpha_ref[...]
    sbk_ref[...] = proj(o, o + sb_w).astype(BF16); o += sb_w
    sbv_ref[...] = proj(o, o + sb_w).astype(BF16); o += sb_w
    log_a = _log_sigmoid(a_pre) * (1.0 / GLA_TAU)
    la_hi = log_a.astype(BF16)
    la_lo = (log_a - la_hi.astype(F32)).astype(BF16)
    tri = tri_ref[...]
    cum_ref[...] = _dot(tri, la_hi) + _dot(tri, la_lo)
    gq_ref[...] = (proj(o, o + kw) * (GLA_DK ** -0.5)).astype(BF16); o += kw
    gk_ref[...] = proj(o, o + kw).astype(BF16); o += kw
    gv_ref[...] = proj(o, o + vw).astype(BF16); o += vw
    r = proj(o, o + vw)
    gr_ref[...] = (r * _sigmoid(r)).astype(BF16)

    d = sgs_ref.shape[-1]
    sgs_ref[...] = _sigmoid(_dot(xn, wgate_ref[:, :d]) + bgate_ref[0:1, :]).astype(BF16)
    sgg_ref[...] = _sigmoid(_dot(xn, wgate_ref[:, d:]) + bgate_ref[1:2, :]).astype(BF16)


def _in_proj(x2, norm_g, w_in, b_gate, w_alpha_up, b_alpha):
    t, d = x2.shape
    sb_w = SB_HEADS * SB_DH
    kw = GLA_HEADS * GLA_DK
    vw = GLA_HEADS * GLA_DV
    n_qkv = 3 * sb_w + 2 * kw + 2 * vw
    wqkv = w_in[:, :n_qkv].astype(BF16)
    wa = jnp.pad(w_in[:, n_qkv:n_qkv + GLA_RANK], ((0, 0), (0, LANES - GLA_RANK))).astype(BF16)
    wgate = w_in[:, n_qkv + GLA_RANK:].astype(BF16)
    wup = jnp.pad(w_alpha_up, ((0, LANES - GLA_RANK), (0, 0))).astype(BF16)
    row = lax.broadcasted_iota(jnp.int32, (TM, TM), 0)
    col = lax.broadcasted_iota(jnp.int32, (TM, TM), 1)
    tri = ((row // GLA_CHUNK == col // GLA_CHUNK) & (col <= row)).astype(BF16)

    tok = lambda w: pl.BlockSpec((TM, w), lambda i: (i, 0))
    out_shapes = [jax.ShapeDtypeStruct((t, w), BF16) for w in (sb_w, sb_w, sb_w, kw, kw, vw, vw)]
    out_shapes += [jax.ShapeDtypeStruct((t, kw), F32),
                   jax.ShapeDtypeStruct((t, d), BF16), jax.ShapeDtypeStruct((t, d), BF16)]
    out_specs = [tok(s.shape[1]) for s in out_shapes]
    return pl.pallas_call(
        _in_proj_kernel,
        grid=(t // TM,),
        in_specs=[tok(d), _const_spec((1, d)), _const_spec(wqkv.shape), _const_spec(wa.shape),
                  _const_spec(wup.shape), _const_spec((1, kw)), _const_spec(wgate.shape),
                  _const_spec((2, d)), _const_spec((TM, TM))],
        out_specs=out_specs,
        out_shape=out_shapes,
        compiler_params=pltpu.CompilerParams(dimension_semantics=("arbitrary",),
                                             vmem_limit_bytes=VMEM_LIMIT),
        name="in_proj",
    )(x2, norm_g.reshape(1, d), wqkv, wa, wup, b_alpha.reshape(1, kw), wgate, b_gate, tri)


SB_STAGES = 4
SB_UNROLL = 12


def _sb_kernel(qtab_ref, ktab_ref, q_ref, k_ref, v_ref, tri_ref, bias_ref, o_ref,
               qm_buf, vm_buf, z_buf, sp_buf, w_buf, carry_ref, acc_ref, *, n_tiles):
    lane = lax.broadcasted_iota(jnp.int32, (1, LANES), 1)
    zero = jnp.zeros((), BF16)
    for hh in range(2):
        hmask = (lane >= hh * SB_DH) & (lane < (hh + 1) * SB_DH)
        qm_buf[hh] = jnp.where(hmask, q_ref[0], zero)
        vm_buf[hh] = jnp.where(hmask, v_ref[0], zero)
    carry_ref[...] = jnp.zeros_like(carry_ref)
    acc_ref[...] = jnp.zeros_like(acc_ref)

    def rows(tile_idx, size):
        return pl.ds(pl.multiple_of(tile_idx * size, size), size)

    def has_tile(i, stage):
        return not isinstance(i, int) or 0 <= i - (stage - 1) < n_tiles

    def step(i, u):
        if has_tile(i, 3):
            a3 = qtab_ref[i - 2]
            p3 = u & 1
            z3 = (u + 2) & 3
            neg_tri = tri_ref[...]
            for hh in range(2):
                x = _dot(sp_buf[p3, hh], neg_tri)
                carry = carry_ref[a3, hh]
                lw = z_buf[z3, hh] + x + jnp.concatenate([carry] * (SB_TK // LANES), axis=1)
                w_buf[p3, hh] = jnp.exp2(lw).astype(BF16)
                carry_ref[a3, hh] = carry + jnp.broadcast_to(x[:, 0:1], (SB_TQ, LANES))

        if has_tile(i, 4):
            q4 = qtab_ref[i - 3]
            k4 = ktab_ref[i - 3]
            p4 = (u + 1) & 1
            pv = (_dot(w_buf[p4, 0], vm_buf[0, rows(k4, SB_TK), :])
                  + _dot(w_buf[p4, 1], vm_buf[1, rows(k4, SB_TK), :]))
            acc = acc_ref[q4] + pv
            acc_ref[q4] = acc
            o_ref[0, rows(q4, SB_TQ), :] = acc.astype(BF16)

        if has_tile(i, 2):
            p2 = (u + 1) & 1
            z2 = (u + 3) & 3
            for hh in range(2):
                z = z_buf[z2, hh]
                sp = jnp.maximum(z, 0.0) + jnp.log(1.0 + jnp.exp2(-jnp.abs(z))) * LOG2_E
                sp_buf[p2, hh] = sp.astype(BF16)

        if has_tile(i, 1):
            q1 = qtab_ref[i]
            k1 = ktab_ref[i]
            z1 = u & 3
            kt = k_ref[0, rows(k1, SB_TK), :]
            bias = bias_ref[(k1 == q1).astype(jnp.int32)]
            for hh in range(2):
                z_buf[z1, hh] = _nt_dot(qm_buf[hh, rows(q1, SB_TQ), :], kt) + bias

    def body(trip, _):
        for u in range(SB_UNROLL):
            step(trip * SB_UNROLL + u, u)
        return 0

    for i in range(SB_UNROLL):
        step(i, i)
    lax.fori_loop(1, n_tiles // SB_UNROLL, body, 0)
    for i in range(n_tiles, n_tiles + SB_STAGES - 1):
        step(i, i % SB_UNROLL)


def _sb_attn(q, k, v):
    b, s, w = q.shape
    nq = s // SB_TQ
    tiles = [(qi, kj) for qi in range(nq) for kj in range(qi, -1, -1)]
    n_tiles = len(tiles)
    assert n_tiles % SB_UNROLL == 0 and SB_UNROLL >= SB_STAGES
    qtab, ktab = (jnp.asarray([t[c] for t in tiles], jnp.int32) for c in range(2))
    row = lax.broadcasted_iota(jnp.int32, (SB_TK, SB_TK), 0)
    col = lax.broadcasted_iota(jnp.int32, (SB_TK, SB_TK), 1)
    tri = -(row >= col).astype(BF16)
    bias = jnp.stack([jnp.zeros((SB_TQ, SB_TK), F32),
                      jnp.where(col < row, 0.0, -1e30).astype(F32)])
    seq_spec = pl.BlockSpec((1, s, LANES), lambda bi, p, *_: (bi, 0, p))
    grid_spec = pltpu.PrefetchScalarGridSpec(
        num_scalar_prefetch=2,
        grid=(b, w // LANES),
        in_specs=[seq_spec, seq_spec, seq_spec,
                  pl.BlockSpec((SB_TK, SB_TK), lambda bi, p, *_: (0, 0)),
                  pl.BlockSpec((2, SB_TQ, SB_TK), lambda bi, p, *_: (0, 0, 0))],
        out_specs=seq_spec,
        scratch_shapes=[pltpu.VMEM((2, s, LANES), BF16),
                        pltpu.VMEM((2, s, LANES), BF16),
                        pltpu.VMEM((4, 2, SB_TQ, SB_TK), F32),
                        pltpu.VMEM((2, 2, SB_TQ, SB_TK), BF16),
                        pltpu.VMEM((2, 2, SB_TQ, SB_TK), BF16),
                        pltpu.VMEM((nq, 2, SB_TQ, LANES), F32),
                        pltpu.VMEM((nq, SB_TQ, LANES), F32)],
    )
    return pl.pallas_call(
        functools.partial(_sb_kernel, n_tiles=n_tiles),
        grid_spec=grid_spec,
        out_shape=jax.ShapeDtypeStruct((b, s, w), BF16),
        compiler_params=pltpu.CompilerParams(dimension_semantics=("arbitrary", "arbitrary"),
                                             vmem_limit_bytes=VMEM_LIMIT),
        name="sb_attn",
    )(qtab, ktab, q, k, v, tri, bias)


def _gla_block(q_ref, k_ref, v_ref, cum_ref, r_ref, g_ref, o_ref, st_ref):
    c = GLA_CHUNK
    lane = lax.broadcasted_iota(jnp.int32, (1, LANES), 1)
    row = lax.broadcasted_iota(jnp.int32, (c, c), 0)
    col = lax.broadcasted_iota(jnp.int32, (c, c), 1)
    causal = col <= row
    for n in range(GLA_BLK // c):
        rows = slice(n * c, (n + 1) * c)
        for p in range(GLA_HEADS // 2):
            lanes = slice(p * LANES, (p + 1) * LANES)
            cum = cum_ref[rows, lanes]
            cum_last = cum[c - 1:c, :]
            qf = q_ref[rows, lanes].astype(F32)
            kf = k_ref[rows, lanes].astype(F32)
            q_dec = qf * jnp.exp(cum)
            k_inv = (kf * jnp.exp(-cum)).astype(BF16)
            k_end = kf * jnp.exp(cum_last - cum)
            decay = jnp.exp(cum_last)
            for hh in range(2):
                h = 2 * p + hh
                hmask = (lane >= hh * GLA_DK) & (lane < (hh + 1) * GLA_DK)
                vlanes = slice(h * GLA_DV, (h + 1) * GLA_DV)
                vb = v_ref[rows, vlanes]
                qd = jnp.where(hmask, q_dec, 0.0).astype(BF16)
                ke = jnp.where(hmask, k_end, 0.0).astype(BF16)
                att = jnp.where(causal, _nt_dot(qd, k_inv), 0.0)
                st = st_ref[h]
                o = _dot(att.astype(BF16), vb) + _nt_dot(qd, st.astype(BF16))
                vt = vb.astype(F32).T.astype(BF16)
                st_ref[h] = st * decay + _dot(vt, ke)
                o = o * lax.rsqrt(jnp.mean(o * o, axis=-1, keepdims=True) + EPS)
                o = o * g_ref[0:1, vlanes] * r_ref[rows, vlanes].astype(F32)
                o_ref[rows, vlanes] = o.astype(BF16)


def _gla_mix_kernel(x_ref, osb_ref, gq_ref, gk_ref, gv_ref, cum_ref, r_ref, g_ref, sgs_ref, sgg_ref,
                    wsb_ref, wgla_ref, wout_ref, h_ref, st_ref, ogla_ref, y_ref, *, tiles_per_seq):
    @pl.when(pl.program_id(0) % tiles_per_seq == 0)
    def _():
        st_ref[...] = jnp.zeros_like(st_ref)

    y_ref[...] = sgs_ref[...].astype(F32) * _dot(osb_ref[...], wsb_ref[...])
    _gla_block(gq_ref, gk_ref, gv_ref, cum_ref, r_ref, g_ref, ogla_ref, st_ref)
    y = y_ref[...] + sgg_ref[...].astype(F32) * _dot(ogla_ref[...], wgla_ref[...])
    h_ref[...] = x_ref[...] + _dot(y.astype(BF16), wout_ref[...])


def _gla_mix(x2, seq_len, o_sb, gq, gk, gv, cum, r_act, norm_g, sg_sb, sg_gla,
             w_branch_sb, w_branch_gla, w_out):
    t, d = x2.shape
    vw = gv.shape[1]
    tok = lambda a: pl.BlockSpec((GLA_BLK, a.shape[1]), lambda i: (i, 0))
    wsb, wgla, wout = (w.astype(BF16) for w in (w_branch_sb, w_branch_gla, w_out))
    tokens = (x2, o_sb, gq, gk, gv, cum, r_act)
    gates = (sg_sb, sg_gla)
    consts = (wsb, wgla, wout)
    return pl.pallas_call(
        functools.partial(_gla_mix_kernel, tiles_per_seq=seq_len // GLA_BLK),
        grid=(t // GLA_BLK,),
        in_specs=([tok(a) for a in tokens] + [_const_spec((1, vw))] + [tok(a) for a in gates]
                  + [_const_spec(w.shape) for w in consts]),
        out_specs=tok(x2),
        out_shape=jax.ShapeDtypeStruct((t, d), F32),
        scratch_shapes=[pltpu.VMEM((GLA_HEADS, GLA_DV, LANES), F32),
                        pltpu.VMEM((GLA_BLK, vw), BF16),
                        pltpu.VMEM((GLA_BLK, d), F32)],
        compiler_params=pltpu.CompilerParams(dimension_semantics=("arbitrary",),
                                             vmem_limit_bytes=VMEM_LIMIT),
        name="gla_mix",
    )(*tokens, norm_g.reshape(1, vw), *gates, *consts)


GELU_C0 = 0.7978845608028654
GELU_K = 0.044715 ** 0.5 / GELU_C0


def _conv_ffn_kernel(h_ref, g_ref, win_ref, cw_ref, cb_ref, wout_ref, gf_ref, o_ref,
                     hn_ref, act_ref, acc_ref, prev_ref, *, tiles_per_seq, final_norm):
    n_ck = wout_ref.shape[0]
    d_ff = n_ck * FF_CK
    tm = h_ref.shape[0]

    @pl.when(pl.program_id(0) % tiles_per_seq == 0)
    def _():
        prev_ref[...] = jnp.zeros_like(prev_ref)

    h = h_ref[...]
    ms = jnp.mean(h * h, axis=-1, keepdims=True)
    hn_ref[...] = (h * lax.rsqrt(ms + EPS) * g_ref[...]).astype(BF16)

    row_blocks = [slice(r, r + FF_RB) for r in range(0, tm, FF_RB)]

    def out_proj(c):
        for rb in row_blocks:
            part = _dot(act_ref[c % 2, rb, :], wout_ref[c])
            if c == 0:
                acc_ref[rb, :] = part
            else:
                acc_ref[rb, :] += part

    for c in range(n_ck):
        hn = hn_ref[...]
        a = _dot(hn, win_ref[:, c * FF_CK:(c + 1) * FF_CK])
        gate = _dot(hn, win_ref[:, d_ff + c * FF_CK:d_ff + (c + 1) * FF_CK])
        if c > 0:
            out_proj(c - 1)
        ext = jnp.concatenate([prev_ref[c], a], axis=0)
        prev_ref[c] = a[tm - SUBLANES:, :]
        a1 = pltpu.roll(ext, 1, 0)[SUBLANES:, :]
        a2 = pltpu.roll(ext, 2, 0)[SUBLANES:, :]
        cw = cw_ref[c]
        for rb in row_blocks:
            y = cb_ref[c] + a2[rb] * cw[0:1, :] + a1[rb] * cw[1:2, :] + a[rb] * cw[2:3, :]
            yb = y.astype(BF16)
            sb = (y * GELU_K).astype(BF16)
            gb = gate[rb].astype(BF16)
            act_ref[c % 2, rb, :] = (yb * gb) * (1.0 + jnp.tanh(yb * (1.0 + sb * sb)))
    out_proj(n_ck - 1)
    h2 = h + acc_ref[...]
    if final_norm:
        ms2 = jnp.mean(h2 * h2, axis=-1, keepdims=True)
        h2 = h2 * lax.rsqrt(ms2 + EPS) * gf_ref[...]
    o_ref[...] = h2


def _conv_ffn(h2d, seq_len, norm_g, w_ffn_in, conv_w, conv_b, w_ffn_out, norm_final_g, final_norm):
    t, d = h2d.shape
    d_ff = w_ffn_out.shape[0]
    n_ck = d_ff // FF_CK
    tok = pl.BlockSpec((FF_TM, d), lambda i: (i, 0))
    chunked =lambda w: w.reshape(w.shape[0], n_ck, FF_CK).transpose(1, 0, 2)
    win = w_ffn_in.astype(BF16)
    wout = ((0.5 / GELU_C0) * w_ffn_out).astype(BF16).reshape(n_ck, FF_CK, d)
    cw = chunked(GELU_C0 * conv_w)
    cb = chunked(GELU_C0 * conv_b.reshape(1, d_ff))
    kern = functools.partial(_conv_ffn_kernel, tiles_per_seq=seq_len // FF_TM,
                             final_norm=final_norm)
    return pl.pallas_call(
        kern,
        grid=(t // FF_TM,),
        in_specs=[tok, _const_spec((1, d)), _const_spec(win.shape), _const_spec(cw.shape),
                  _const_spec(cb.shape), _const_spec(wout.shape), _const_spec((1, d))],
        out_specs=tok,
        out_shape=jax.ShapeDtypeStruct((t, d), F32),
        scratch_shapes=[pltpu.VMEM((FF_TM, d), BF16),
                        pltpu.VMEM((2, FF_TM, FF_CK), BF16),
                        pltpu.VMEM((FF_TM, d), F32),
                        pltpu.VMEM((n_ck, SUBLANES, FF_CK), F32)],
        compiler_params=pltpu.CompilerParams(dimension_semantics=("arbitrary",),
                                             vmem_limit_bytes=VMEM_LIMIT),
        name="conv_ffn",
    )(h2d, norm_g.reshape(1, d), win, cw, cb, wout, norm_final_g.reshape(1, d))


def kernel(x, norm_mix_g, w_in, b_gate, w_alpha_up, b_alpha, gla_norm_g, w_branch_sb, w_branch_gla,
           w_out, norm_ffn_g, w_ffn_in, conv_w, conv_b, w_ffn_out, norm_final_g):
    b, s, d = x.shape
    depth = w_in.shape[0]
    assert s % TM == 0 and s % SB_TQ == 0 and s % GLA_BLK == 0 and (b * s) % TM == 0
    assert w_ffn_out.shape[1] % FF_CK == 0 and conv_w.shape[1] == CONV_WIDTH
    assert SB_UNROLL % 4 == 0 and FF_TM % FF_RB == 0 and s % FF_TM == 0
    h = x.reshape(b * s, d)
    for i in range(depth):
        (sbq, sbk, sbv, gq, gk, gv, r_act, cum, sg_sb, sg_gla) = _in_proj(
            h, norm_mix_g[i], w_in[i], b_gate[i], w_alpha_up[i], b_alpha[i])
        seq = lambda a: a.reshape(b, s, a.shape[-1])
        o_sb = _sb_attn(seq(sbq), seq(sbk), seq(sbv)).reshape(b * s, -1)
        h = _gla_mix(h, s, o_sb, gq, gk, gv, cum, r_act, gla_norm_g[i], sg_sb, sg_gla,
                     w_branch_sb[i], w_branch_gla[i], w_out[i])
        h = _conv_ffn(h, s, norm_ffn_g[i], w_ffn_in[i], conv_w[i], conv_b[i], w_ffn_out[i],
                      norm_final_g, final_norm=(i == depth - 1))
    return h.reshape(b, s, d)
```

```python
import functools

import jax
import jax.numpy as jnp
from jax import lax
from jax.experimental import pallas as pl
from jax.experimental.pallas import tpu as pltpu

F32 = jnp.float32
BF16 = jnp.bfloat16

EPS = 1e-6
LOG2_E = 1.4426950408889634
SB_HEADS = 8
SB_DH = 64
GLA_HEADS = 4
GLA_DK = 64
GLA_DV = 128
GLA_RANK = 16
GLA_TAU = 16.0
GLA_CHUNK = 64
CONV_WIDTH = 3

LANES = 128
SUBLANES = 8
VMEM_LIMIT = 56 * 1024 * 1024

IN_TM = 256
IN_SUB = 2
FF_TM = 256
FF_SUB = 2
SB_TQ = 256
SB_TK = 256
GLA_BLK = 512
FF_CK = 256


def _nt_dot(a, b):
    return lax.dot_general(a, b, (((1,), (1,)), ((), ())), preferred_element_type=F32)


def _dot(a, b):
    return jnp.dot(a, b, preferred_element_type=F32)


def _log_sigmoid(z):
    return jnp.minimum(z, 0.0) - jnp.log(1.0 + jnp.exp(-jnp.abs(z)))


def _sigmoid(z):
    return 1.0 / (1.0 + jnp.exp(-z))


def _const_spec(shape):
    return pl.BlockSpec(shape, lambda *_: (0,) * len(shape))


def _in_proj_kernel(x_ref, g_ref, wqkv_ref, wa_ref, wup_ref, balpha_ref, wgate_ref, bgate_ref,
                    tri_ref, sbq_ref, sbk_ref, sbv_ref, gq_ref, gk_ref, gv_ref, gr_ref, cum_ref,
                    sgs_ref, sgg_ref):
    sb_w = SB_HEADS * SB_DH
    kw = GLA_HEADS * GLA_DK
    vw = GLA_HEADS * GLA_DV
    d = sgs_ref.shape[-1]

    def tile(rows):
        x = x_ref[rows, :]
        ms = jnp.mean(x * x, axis=-1, keepdims=True)
        xn = (x * lax.rsqrt(ms + EPS) * g_ref[...]).astype(BF16)

        def proj(lo, hi):
            return _dot(xn, wqkv_ref[:, lo:hi])

        code = _dot(xn, wa_ref[...])
        o = 0
        sbq_ref[rows, :] = (proj(o, o + sb_w) * (SB_DH ** -0.5 * LOG2_E)).astype(BF16); o += sb_w
        a_pre = _dot(code.astype(BF16), wup_ref[...]) + balpha_ref[...]
        sbk_ref[rows, :] = proj(o, o + sb_w).astype(BF16); o += sb_w
        sbv_ref[rows, :] = proj(o, o + sb_w).astype(BF16); o += sb_w
        log_a = _log_sigmoid(a_pre) * (1.0 / GLA_TAU)
        la_hi = log_a.astype(BF16)
        la_lo = (log_a - la_hi.astype(F32)).astype(BF16)
        tri = tri_ref[...]
        cum_ref[rows, :] = _dot(tri, la_hi) + _dot(tri, la_lo)
        gq_ref[rows, :] = (proj(o, o + kw) * (GLA_DK ** -0.5)).astype(BF16); o += kw
        gk_ref[rows, :] = proj(o, o + kw).astype(BF16); o += kw
        gv_ref[rows, :] = proj(o, o + vw).astype(BF16); o += vw
        r = proj(o, o + vw)
        gr_ref[rows, :] = (r * _sigmoid(r)).astype(BF16)
        sgs_ref[rows, :] = _sigmoid(_dot(xn, wgate_ref[:, :d]) + bgate_ref[0:1, :]).astype(BF16)
        sgg_ref[rows, :] = _sigmoid(_dot(xn, wgate_ref[:, d:]) + bgate_ref[1:2, :]).astype(BF16)

    for sub in range(IN_SUB):
        tile(slice(sub * IN_TM, (sub + 1) * IN_TM))


def _in_proj(x2, norm_g, w_in, b_gate, w_alpha_up, b_alpha):
    t, d = x2.shape
    sb_w = SB_HEADS * SB_DH
    kw = GLA_HEADS * GLA_DK
    vw = GLA_HEADS * GLA_DV
    n_qkv = 3 * sb_w + 2 * kw + 2 * vw
    wqkv = w_in[:, :n_qkv].astype(BF16)
    wa = jnp.pad(w_in[:, n_qkv:n_qkv + GLA_RANK], ((0, 0), (0, LANES - GLA_RANK))).astype(BF16)
    wgate = w_in[:, n_qkv + GLA_RANK:].astype(BF16)
    wup = jnp.pad(w_alpha_up, ((0, LANES - GLA_RANK), (0, 0))).astype(BF16)
    row = lax.broadcasted_iota(jnp.int32, (IN_TM, IN_TM), 0)
    col = lax.broadcasted_iota(jnp.int32, (IN_TM, IN_TM), 1)
    tri = ((row // GLA_CHUNK == col // GLA_CHUNK) & (col <= row)).astype(BF16)

    blk = IN_SUB * IN_TM
    tok = lambda w: pl.BlockSpec((blk, w), lambda i: (i, 0))
    out_shapes = [jax.ShapeDtypeStruct((t, w), BF16) for w in (sb_w, sb_w, sb_w, kw, kw, vw, vw)]
    out_shapes += [jax.ShapeDtypeStruct((t, kw), F32),
                   jax.ShapeDtypeStruct((t, d), BF16), jax.ShapeDtypeStruct((t, d), BF16)]
    out_specs = [tok(s.shape[1]) for s in out_shapes]
    return pl.pallas_call(
        _in_proj_kernel,
        grid=(t // blk,),
        in_specs=[tok(d), _const_spec((1, d)), _const_spec(wqkv.shape), _const_spec(wa.shape),
                  _const_spec(wup.shape), _const_spec((1, kw)), _const_spec(wgate.shape),
                  _const_spec((2, d)), _const_spec((IN_TM, IN_TM))],
        out_specs=out_specs,
        out_shape=out_shapes,
        compiler_params=pltpu.CompilerParams(dimension_semantics=("arbitrary",),
                                             vmem_limit_bytes=VMEM_LIMIT),
        name="in_proj",
    )(x2, norm_g.reshape(1, d), wqkv, wa, wup, b_alpha.reshape(1, kw), wgate, b_gate, tri)


SB_STAGES = 4
SB_UNROLL = 12


def _sb_kernel(qtab_ref, ktab_ref, q_ref, k_ref, v_ref, tri_ref, bias_ref, o_ref,
               qm_buf, vm_buf, z_buf, sp_buf, w_buf, carry_ref, acc_ref, *, n_tiles):
    lane = lax.broadcasted_iota(jnp.int32, (1, LANES), 1)
    zero = jnp.zeros((), BF16)
    for hh in range(2):
        hmask = (lane >= hh * SB_DH) & (lane < (hh + 1) * SB_DH)
        qm_buf[hh] = jnp.where(hmask, q_ref[0], zero)
        vm_buf[hh] = jnp.where(hmask, v_ref[0], zero)
    carry_ref[...] = jnp.zeros_like(carry_ref)
    acc_ref[...] = jnp.zeros_like(acc_ref)

    def rows(tile_idx, size):
        return pl.ds(pl.multiple_of(tile_idx * size, size), size)

    def has_tile(i, stage):
        return not isinstance(i, int) or 0 <= i - (stage - 1) < n_tiles

    def step(i, u):
        if has_tile(i, 3):
            a3 = qtab_ref[i - 2]
            p3 = u & 1
            z3 = (u + 2) & 3
            neg_tri = tri_ref[...]
            for hh in range(2):
                x = _dot(sp_buf[p3, hh], neg_tri)
                carry = carry_ref[a3, hh]
                lw = z_buf[z3, hh] + x + jnp.concatenate([carry] * (SB_TK // LANES), axis=1)
                w_buf[p3, hh] = jnp.exp2(lw).astype(BF16)
                carry_ref[a3, hh] = carry + jnp.broadcast_to(x[:, 0:1], (SB_TQ, LANES))

        if has_tile(i, 4):
            q4 = qtab_ref[i - 3]
            k4 = ktab_ref[i - 3]
            p4 = (u + 1) & 1
            pv = (_dot(w_buf[p4, 0], vm_buf[0, rows(k4, SB_TK), :])
                  + _dot(w_buf[p4, 1], vm_buf[1, rows(k4, SB_TK), :]))
            acc = acc_ref[q4] + pv
            acc_ref[q4] = acc
            o_ref[0, rows(q4, SB_TQ), :] = acc.astype(BF16)

        if has_tile(i, 2):
            p2 = (u + 1) & 1
            z2 = (u + 3) & 3
            for hh in range(2):
                z = z_buf[z2, hh]
                sp = jnp.maximum(z, 0.0) + jnp.log(1.0 + jnp.exp2(-jnp.abs(z))) * LOG2_E
                sp_buf[p2, hh] = sp.astype(BF16)

        if has_tile(i, 1):
            q1 = qtab_ref[i]
            k1 = ktab_ref[i]
            z1 = u & 3
            kt = k_ref[0, rows(k1, SB_TK), :]
            bias = bias_ref[(k1 == q1).astype(jnp.int32)]
            for hh in range(2):
                z_buf[z1, hh] = _nt_dot(qm_buf[hh, rows(q1, SB_TQ), :], kt) + bias

    def body(trip, _):
        for u in range(SB_UNROLL):
            step(trip * SB_UNROLL + u, u)
        return 0

    for i in range(SB_UNROLL):
        step(i, i)
    lax.fori_loop(1, n_tiles // SB_UNROLL, body, 0)
    for i in range(n_tiles, n_tiles + SB_STAGES - 1):
        step(i, i % SB_UNROLL)


def _sb_attn(q, k, v):
    b, s, w = q.shape
    nq = s // SB_TQ
    tiles = [(qi, kj) for qi in range(nq) for kj in range(qi, -1, -1)]
    n_tiles = len(tiles)
    assert n_tiles % SB_UNROLL == 0 and SB_UNROLL >= SB_STAGES
    qtab, ktab = (jnp.asarray([t[c] for t in tiles], jnp.int32) for c in range(2))
    row = lax.broadcasted_iota(jnp.int32, (SB_TK, SB_TK), 0)
    col = lax.broadcasted_iota(jnp.int32, (SB_TK, SB_TK), 1)
    tri = -(row >= col).astype(BF16)
    bias = jnp.stack([jnp.zeros((SB_TQ, SB_TK), F32),
                      jnp.where(col < row, 0.0, -1e30).astype(F32)])
    seq_spec = pl.BlockSpec((1, s, LANES), lambda bi, p, *_: (bi, 0, p))
    grid_spec = pltpu.PrefetchScalarGridSpec(
        num_scalar_prefetch=2,
        grid=(b, w // LANES),
        in_specs=[seq_spec, seq_spec, seq_spec,
                  pl.BlockSpec((SB_TK, SB_TK), lambda bi, p, *_: (0, 0)),
                  pl.BlockSpec((2, SB_TQ, SB_TK), lambda bi, p, *_: (0, 0, 0))],
        out_specs=seq_spec,
        scratch_shapes=[pltpu.VMEM((2, s, LANES), BF16),
                        pltpu.VMEM((2, s, LANES), BF16),
                        pltpu.VMEM((4, 2, SB_TQ, SB_TK), F32),
                        pltpu.VMEM((2, 2, SB_TQ, SB_TK), BF16),
                        pltpu.VMEM((2, 2, SB_TQ, SB_TK), BF16),
                        pltpu.VMEM((nq, 2, SB_TQ, LANES), F32),
                        pltpu.VMEM((nq, SB_TQ, LANES), F32)],
    )
    return pl.pallas_call(
        functools.partial(_sb_kernel, n_tiles=n_tiles),
        grid_spec=grid_spec,
        out_shape=jax.ShapeDtypeStruct((b, s, w), BF16),
        compiler_params=pltpu.CompilerParams(dimension_semantics=("arbitrary", "arbitrary"),
                                             vmem_limit_bytes=VMEM_LIMIT),
        name="sb_attn",
    )(qtab, ktab, q, k, v, tri, bias)


def _gla_block(q_ref, k_ref, v_ref, cum_ref, r_ref, g_ref, o_ref, st_ref):
    c = GLA_CHUNK
    lane = lax.broadcasted_iota(jnp.int32, (1, LANES), 1)
    row = lax.broadcasted_iota(jnp.int32, (c, c), 0)
    col = lax.broadcasted_iota(jnp.int32, (c, c), 1)
    causal = col <= row
    for n in range(GLA_BLK // c):
        rows = slice(n * c, (n + 1) * c)
        for p in range(GLA_HEADS // 2):
            lanes = slice(p * LANES, (p + 1) * LANES)
            cum = cum_ref[rows, lanes]
            cum_last = cum[c - 1:c, :]
            qf = q_ref[rows, lanes].astype(F32)
            kf = k_ref[rows, lanes].astype(F32)
            q_dec = qf * jnp.exp(cum)
            k_inv = (kf * jnp.exp(-cum)).astype(BF16)
            k_end = kf * jnp.exp(cum_last - cum)
            decay = jnp.exp(cum_last)
            for hh in range(2):
                h = 2 * p + hh
                hmask = (lane >= hh * GLA_DK) & (lane < (hh + 1) * GLA_DK)
                vlanes = slice(h * GLA_DV, (h + 1) * GLA_DV)
                vb = v_ref[rows, vlanes]
                qd = jnp.where(hmask, q_dec, 0.0).astype(BF16)
                ke = jnp.where(hmask, k_end, 0.0).astype(BF16)
                att = jnp.where(causal, _nt_dot(qd, k_inv), 0.0)
                st = st_ref[h]
                o = _dot(att.astype(BF16), vb) + _nt_dot(qd, st.astype(BF16))
                vt = vb.astype(F32).T.astype(BF16)
                st_ref[h] = st * decay + _dot(vt, ke)
                o = o * lax.rsqrt(jnp.mean(o * o, axis=-1, keepdims=True) + EPS)
                o = o * g_ref[0:1, vlanes] * r_ref[rows, vlanes].astype(F32)
                o_ref[rows, vlanes] = o.astype(BF16)


def _gla_mix_kernel(x_ref, osb_ref, gq_ref, gk_ref, gv_ref, cum_ref, r_ref, g_ref, sgs_ref, sgg_ref,
                    wsb_ref, wgla_ref, wout_ref, h_ref, st_ref, ogla_ref, y_ref, *, tiles_per_seq):
    @pl.when(pl.program_id(0) % tiles_per_seq == 0)
    def _():
        st_ref[...] = jnp.zeros_like(st_ref)

    y_ref[...] = sgs_ref[...].astype(F32) * _dot(osb_ref[...], wsb_ref[...])
    _gla_block(gq_ref, gk_ref, gv_ref, cum_ref, r_ref, g_ref, ogla_ref, st_ref)
    y = y_ref[...] + sgg_ref[...].astype(F32) * _dot(ogla_ref[...], wgla_ref[...])
    h_ref[...] = x_ref[...] + _dot(y.astype(BF16), wout_ref[...])


def _gla_mix(x2, seq_len, o_sb, gq, gk, gv, cum, r_act, norm_g, sg_sb, sg_gla,
             w_branch_sb, w_branch_gla, w_out):
    t, d = x2.shape
    vw = gv.shape[1]
    tok = lambda a: pl.BlockSpec((GLA_BLK, a.shape[1]), lambda i: (i, 0))
    wsb, wgla, wout = (w.astype(BF16) for w in (w_branch_sb, w_branch_gla, w_out))
    tokens = (x2, o_sb, gq, gk, gv, cum, r_act)
    gates = (sg_sb, sg_gla)
    consts = (wsb, wgla, wout)
    return pl.pallas_call(
        functools.partial(_gla_mix_kernel, tiles_per_seq=seq_len // GLA_BLK),
        grid=(t // GLA_BLK,),
        in_specs=([tok(a) for a in tokens] + [_const_spec((1, vw))] + [tok(a) for a in gates]
                  + [_const_spec(w.shape) for w in consts]),
        out_specs=tok(x2),
        out_shape=jax.ShapeDtypeStruct((t, d), F32),
        scratch_shapes=[pltpu.VMEM((GLA_HEADS, GLA_DV, LANES), F32),
                        pltpu.VMEM((GLA_BLK, vw), BF16),
                        pltpu.VMEM((GLA_BLK, d), F32)],
        compiler_params=pltpu.CompilerParams(dimension_semantics=("arbitrary",),
                                             vmem_limit_bytes=VMEM_LIMIT),
        name="gla_mix",
    )(*tokens, norm_g.reshape(1, vw), *gates, *consts)


GELU_C0 = 0.7978845608028654
GELU_K = 0.044715 ** 0.5 / GELU_C0


def _conv_ffn_kernel(h_ref, g_ref, win_ref, cw_ref, cb_ref, wout_ref, gf_ref, o_ref,
                     hn_ref, act_ref, acc_ref, prev_ref, *, tiles_per_seq, final_norm):
    n_ck = wout_ref.shape[0]
    d_ff = n_ck * FF_CK
    tm = FF_TM

    @pl.when(pl.program_id(0) % tiles_per_seq == 0)
    def _():
        prev_ref[...] = jnp.zeros_like(prev_ref)

    def tile(sub):
        rows = slice(sub * tm, (sub + 1) * tm)
        h = h_ref[rows, :]
        ms = jnp.mean(h * h, axis=-1, keepdims=True)
        hn_ref[sub] = (h * lax.rsqrt(ms + EPS) * g_ref[...]).astype(BF16)

        def out_proj(c):
            part = _dot(act_ref[sub, c % 2], wout_ref[c])
            if c == 0:
                acc_ref[sub] = part
            else:
                acc_ref[sub] += part

        for c in range(n_ck):
            hn = hn_ref[sub]
            a = _dot(hn, win_ref[:, c * FF_CK:(c + 1) * FF_CK])
            gate = _dot(hn, win_ref[:, d_ff + c * FF_CK:d_ff + (c + 1) * FF_CK])
            if c > 0:
                out_proj(c - 1)
            ext = jnp.concatenate([prev_ref[c], a], axis=0)
            prev_ref[c] = a[tm - SUBLANES:, :]
            a1 = pltpu.roll(ext, 1, 0)[SUBLANES:, :]
            a2 = pltpu.roll(ext, 2, 0)[SUBLANES:, :]
            cw = cw_ref[c]
            y = cb_ref[c] + a2 * cw[0:1, :] + a1 * cw[1:2, :] + a * cw[2:3, :]
            yb = y.astype(BF16)
            sb = (y * GELU_K).astype(BF16)
            gb = gate.astype(BF16)
            act_ref[sub, c % 2] = (yb * gb) * (1.0 + jnp.tanh(yb * (1.0 + sb * sb)))
        out_proj(n_ck - 1)
        h2 = h + acc_ref[sub]
        if final_norm:
            ms2 = jnp.mean(h2 * h2, axis=-1, keepdims=True)
            h2 = h2 * lax.rsqrt(ms2 + EPS) * gf_ref[...]
        o_ref[rows, :] = h2

    for sub in range(FF_SUB):
        tile(sub)


def _conv_ffn(h2d, seq_len, norm_g, w_ffn_in, conv_w, conv_b, w_ffn_out, norm_final_g, final_norm):
    t, d = h2d.shape
    d_ff = w_ffn_out.shape[0]
    n_ck = d_ff // FF_CK
    blk = FF_SUB * FF_TM
    tok = pl.BlockSpec((blk, d), lambda i: (i, 0))
    chunked = lambda w: w.reshape(w.shape[0], n_ck, FF_CK).transpose(1, 0, 2)
    win = w_ffn_in.astype(BF16)
    wout = ((0.5 / GELU_C0) * w_ffn_out).astype(BF16).reshape(n_ck, FF_CK, d)
    cw = chunked(GELU_C0 * conv_w)
    cb = chunked(GELU_C0 * conv_b.reshape(1, d_ff))
    kern = functools.partial(_conv_ffn_kernel, tiles_per_seq=seq_len // blk, final_norm=final_norm)
    return pl.pallas_call(
        kern,
        grid=(t // blk,),
        in_specs=[tok, _const_spec((1, d)), _const_spec(win.shape), _const_spec(cw.shape),
                  _const_spec(cb.shape), _const_spec(wout.shape), _const_spec((1, d))],
        out_specs=tok,
        out_shape=jax.ShapeDtypeStruct((t, d), F32),
        scratch_shapes=[pltpu.VMEM((FF_SUB, FF_TM, d), BF16),
                        pltpu.VMEM((FF_SUB, 2, FF_TM, FF_CK), BF16),
                        pltpu.VMEM((FF_SUB, FF_TM, d), F32),
                        pltpu.VMEM((n_ck, SUBLANES, FF_CK), F32)],
        compiler_params=pltpu.CompilerParams(dimension_semantics=("arbitrary",),
                                             vmem_limit_bytes=VMEM_LIMIT),
        name="conv_ffn",
    )(h2d, norm_g.reshape(1, d), win, cw, cb, wout, norm_final_g.reshape(1, d))


def kernel(x, norm_mix_g, w_in, b_gate, w_alpha_up, b_alpha, gla_norm_g, w_branch_sb, w_branch_gla,
           w_out, norm_ffn_g, w_ffn_in, conv_w, conv_b, w_ffn_out, norm_final_g):
    b, s, d = x.shape
    depth = w_in.shape[0]
    assert s % (IN_SUB * IN_TM) == 0 and IN_TM % GLA_CHUNK == 0
    assert s % SB_TQ == 0 and s % GLA_BLK == 0
    assert w_ffn_out.shape[1] % FF_CK == 0 and conv_w.shape[1] == CONV_WIDTH
    assert SB_UNROLL % 4 == 0 and s % (FF_SUB * FF_TM) == 0
    h = x.reshape(b * s, d)
    for i in range(depth):
        (sbq, sbk, sbv, gq, gk, gv, r_act, cum, sg_sb, sg_gla) = _in_proj(
            h, norm_mix_g[i], w_in[i], b_gate[i], w_alpha_up[i], b_alpha[i])
        seq = lambda a: a.reshape(b, s, a.shape[-1])
        o_sb = _sb_attn(seq(sbq), seq(sbk), seq(sbv)).reshape(b * s, -1)
        h = _gla_mix(h, s, o_sb, gq, gk, gv, cum, r_act, gla_norm_g[i], sg_sb, sg_gla,
                     w_branch_sb[i], w_branch_gla[i], w_out[i])
        h = _conv_ffn(h, s, norm_ffn_g[i], w_ffn_in[i], conv_w[i], conv_b[i], w_ffn_out[i],
                      norm_final_g, final_norm=(i == depth - 1))
    return h.reshape(b, s, d)
```

```python
import functools

import jax
import jax.numpy as jnp
from jax import lax
from jax.experimental import pallas as pl
from jax.experimental.pallas import tpu as pltpu

F32 = jnp.float32
BF16 = jnp.bfloat16

EPS = 1e-6
LOG2_E = 1.4426950408889634
SB_HEADS = 8
SB_DH = 64
GLA_HEADS = 4
GLA_DK = 64
GLA_DV = 128
GLA_RANK = 16
GLA_TAU = 16.0
GLA_CHUNK = 64
CONV_WIDTH = 3

LANES = 128
SUBLANES = 8
VMEM_LIMIT = 56 * 1024 * 1024

IN_TM = 256
IN_SUB = 2
FF_TM = 256
FF_SUB = 2
SB_TQ = 256
SB_TK = 256
GLA_BLK = 512
GLA_GROUP = 2
FF_CK = 256


def _nt_dot(a, b):
    return lax.dot_general(a, b, (((1,), (1,)), ((), ())), preferred_element_type=F32)


def _dot(a, b):
    return jnp.dot(a, b, preferred_element_type=F32)


def _log_sigmoid(z):
    return jnp.minimum(z, 0.0) - jnp.log(1.0 + jnp.exp(-jnp.abs(z)))


def _sigmoid(z):
    return 1.0 / (1.0 + jnp.exp(-z))


def _const_spec(shape):
    return pl.BlockSpec(shape, lambda *_: (0,) * len(shape))


def _in_proj_kernel(x_ref, g_ref, wqkv_ref, wa_ref, wup_ref, balpha_ref, wgate_ref, bgate_ref,
                    tri_ref, sbq_ref, sbk_ref, sbv_ref, gq_ref, gk_ref, gv_ref, gr_ref, cum_ref,
                    sgs_ref, sgg_ref):
    sb_w = SB_HEADS * SB_DH
    kw = GLA_HEADS * GLA_DK
    vw = GLA_HEADS * GLA_DV
    d = sgs_ref.shape[-1]

    def tile(rows):
        x = x_ref[rows, :]
        ms = jnp.mean(x * x, axis=-1, keepdims=True)
        xn = (x * lax.rsqrt(ms + EPS) * g_ref[...]).astype(BF16)

        def proj(lo, hi):
            return _dot(xn, wqkv_ref[:, lo:hi])

        code = _dot(xn, wa_ref[...])
        o = 0
        sbq_ref[rows, :] = (proj(o, o + sb_w) * (SB_DH ** -0.5 * LOG2_E)).astype(BF16); o += sb_w
        a_pre = _dot(code.astype(BF16), wup_ref[...]) + balpha_ref[...]
        sbk_ref[rows, :] = proj(o, o + sb_w).astype(BF16); o += sb_w
        sbv_ref[rows, :] = proj(o, o + sb_w).astype(BF16); o += sb_w
        log_a = _log_sigmoid(a_pre) * (1.0 / GLA_TAU)
        la_hi = log_a.astype(BF16)
        la_lo = (log_a - la_hi.astype(F32)).astype(BF16)
        tri = tri_ref[...]
        cum_ref[rows, :] = _dot(tri, la_hi) + _dot(tri, la_lo)
        gq_ref[rows, :] = (proj(o, o + kw) * (GLA_DK ** -0.5)).astype(BF16); o += kw
        gk_ref[rows, :] = proj(o, o + kw).astype(BF16); o += kw
        gv_ref[rows, :] = proj(o, o + vw).astype(BF16); o += vw
        r = proj(o, o + vw)
        gr_ref[rows, :] = (r * _sigmoid(r)).astype(BF16)
        sgs_ref[rows, :] = _sigmoid(_dot(xn, wgate_ref[:, :d]) + bgate_ref[0:1, :]).astype(BF16)
        sgg_ref[rows, :] = _sigmoid(_dot(xn, wgate_ref[:, d:]) + bgate_ref[1:2, :]).astype(BF16)

    for sub in range(IN_SUB):
        tile(slice(sub * IN_TM, (sub + 1) * IN_TM))


def _in_proj(x2, norm_g, w_in, b_gate, w_alpha_up, b_alpha):
    t, d = x2.shape
    sb_w = SB_HEADS * SB_DH
    kw = GLA_HEADS * GLA_DK
    vw = GLA_HEADS * GLA_DV
    n_qkv = 3 * sb_w + 2 * kw + 2 * vw
    wqkv = w_in[:, :n_qkv].astype(BF16)
    wa = jnp.pad(w_in[:, n_qkv:n_qkv + GLA_RANK], ((0, 0), (0, LANES - GLA_RANK))).astype(BF16)
    wgate = w_in[:, n_qkv + GLA_RANK:].astype(BF16)
    wup = jnp.pad(w_alpha_up, ((0, LANES - GLA_RANK), (0, 0))).astype(BF16)
    row = lax.broadcasted_iota(jnp.int32, (IN_TM, IN_TM), 0)
    col = lax.broadcasted_iota(jnp.int32, (IN_TM, IN_TM), 1)
    tri = ((row // GLA_CHUNK == col // GLA_CHUNK) & (col <= row)).astype(BF16)

    blk = IN_SUB * IN_TM
    tok = lambda w: pl.BlockSpec((blk, w), lambda i: (i, 0))
    out_shapes = [jax.ShapeDtypeStruct((t, w), BF16) for w in (sb_w, sb_w, sb_w, kw, kw, vw, vw)]
    out_shapes += [jax.ShapeDtypeStruct((t, kw), F32),
                   jax.ShapeDtypeStruct((t, d), BF16), jax.ShapeDtypeStruct((t, d), BF16)]
    out_specs = [tok(s.shape[1]) for s in out_shapes]
    return pl.pallas_call(
        _in_proj_kernel,
        grid=(t // blk,),
        in_specs=[tok(d), _const_spec((1, d)), _const_spec(wqkv.shape), _const_spec(wa.shape),
                  _const_spec(wup.shape), _const_spec((1, kw)), _const_spec(wgate.shape),
                  _const_spec((2, d)), _const_spec((IN_TM, IN_TM))],
        out_specs=out_specs,
        out_shape=out_shapes,
        compiler_params=pltpu.CompilerParams(dimension_semantics=("arbitrary",),
                                             vmem_limit_bytes=VMEM_LIMIT),
        name="in_proj",
    )(x2, norm_g.reshape(1, d), wqkv, wa, wup, b_alpha.reshape(1, kw), wgate, b_gate, tri)


SB_STAGES = 4
SB_UNROLL = 12


def _sb_kernel(qtab_ref, ktab_ref, q_ref, k_ref, v_ref, tri_ref, bias_ref, o_ref,
               qm_buf, vm_buf, z_buf, sp_buf, w_buf, carry_ref, acc_ref, *, n_tiles):
    lane = lax.broadcasted_iota(jnp.int32, (1, LANES), 1)
    zero = jnp.zeros((), BF16)
    for hh in range(2):
        hmask = (lane >= hh * SB_DH) & (lane < (hh + 1) * SB_DH)
        qm_buf[hh] = jnp.where(hmask, q_ref[0], zero)
        vm_buf[hh] = jnp.where(hmask, v_ref[0], zero)
    carry_ref[...] = jnp.zeros_like(carry_ref)
    acc_ref[...] = jnp.zeros_like(acc_ref)

    def rows(tile_idx, size):
        return pl.ds(pl.multiple_of(tile_idx * size, size), size)

    def has_tile(i, stage):
        return not isinstance(i, int) or 0 <= i - (stage - 1) < n_tiles

    def step(i, u):
        if has_tile(i, 3):
            a3 = qtab_ref[i - 2]
            p3 = u & 1
            z3 = (u + 2) & 3
            neg_tri = tri_ref[...]
            for hh in range(2):
                x = _dot(sp_buf[p3, hh], neg_tri)
                carry = carry_ref[a3, hh]
                lw = z_buf[z3, hh] + x + jnp.concatenate([carry] * (SB_TK // LANES), axis=1)
                w_buf[p3, hh] = jnp.exp2(lw).astype(BF16)
                carry_ref[a3, hh] = carry + jnp.broadcast_to(x[:, 0:1], (SB_TQ, LANES))

        if has_tile(i, 4):
            q4 = qtab_ref[i - 3]
            k4 = ktab_ref[i - 3]
            p4 = (u + 1) & 1
            pv = (_dot(w_buf[p4, 0], vm_buf[0, rows(k4, SB_TK), :])
                  + _dot(w_buf[p4, 1], vm_buf[1, rows(k4, SB_TK), :]))
            acc = acc_ref[q4] + pv
            acc_ref[q4] = acc
            o_ref[0, rows(q4, SB_TQ), :] = acc.astype(BF16)

        if has_tile(i, 2):
            p2 = (u + 1) & 1
            z2 = (u + 3) & 3
            for hh in range(2):
                z = z_buf[z2, hh]
                sp = jnp.maximum(z, 0.0) + jnp.log(1.0 + jnp.exp2(-jnp.abs(z))) * LOG2_E
                sp_buf[p2, hh] = sp.astype(BF16)

        if has_tile(i, 1):
            q1 = qtab_ref[i]
            k1 = ktab_ref[i]
            z1 = u & 3
            kt = k_ref[0, rows(k1, SB_TK), :]
            bias = bias_ref[(k1 == q1).astype(jnp.int32)]
            for hh in range(2):
                z_buf[z1, hh] = _nt_dot(qm_buf[hh, rows(q1, SB_TQ), :], kt) + bias

    def body(trip, _):
        for u in range(SB_UNROLL):
            step(trip * SB_UNROLL + u, u)
        return 0

    for i in range(SB_UNROLL):
        step(i, i)
    lax.fori_loop(1, n_tiles // SB_UNROLL, body, 0)
    for i in range(n_tiles, n_tiles + SB_STAGES - 1):
        step(i, i % SB_UNROLL)


def _sb_attn(q, k, v):
    b, s, w = q.shape
    nq = s // SB_TQ
    tiles = [(qi, kj) for qi in range(nq) for kj in range(qi, -1, -1)]
    n_tiles = len(tiles)
    assert n_tiles % SB_UNROLL == 0 and SB_UNROLL >= SB_STAGES
    qtab, ktab = (jnp.asarray([t[c] for t in tiles], jnp.int32) for c in range(2))
    row = lax.broadcasted_iota(jnp.int32, (SB_TK, SB_TK), 0)
    col = lax.broadcasted_iota(jnp.int32, (SB_TK, SB_TK), 1)
    tri = -(row >= col).astype(BF16)
    bias = jnp.stack([jnp.zeros((SB_TQ, SB_TK), F32),
                      jnp.where(col < row, 0.0, -1e30).astype(F32)])
    seq_spec = pl.BlockSpec((1, s, LANES), lambda bi, p, *_: (bi, 0, p))
    grid_spec = pltpu.PrefetchScalarGridSpec(
        num_scalar_prefetch=2,
        grid=(b, w // LANES),
        in_specs=[seq_spec, seq_spec, seq_spec,
                  pl.BlockSpec((SB_TK, SB_TK), lambda bi, p, *_: (0, 0)),
                  pl.BlockSpec((2, SB_TQ, SB_TK), lambda bi, p, *_: (0, 0, 0))],
        out_specs=seq_spec,
        scratch_shapes=[pltpu.VMEM((2, s, LANES), BF16),
                        pltpu.VMEM((2, s, LANES), BF16),
                        pltpu.VMEM((4, 2, SB_TQ, SB_TK), F32),
                        pltpu.VMEM((2, 2, SB_TQ, SB_TK), BF16),
                        pltpu.VMEM((2, 2, SB_TQ, SB_TK), BF16),
                        pltpu.VMEM((nq, 2, SB_TQ, LANES), F32),
                        pltpu.VMEM((nq, SB_TQ, LANES), F32)],
    )
    return pl.pallas_call(
        functools.partial(_sb_kernel, n_tiles=n_tiles),
        grid_spec=grid_spec,
        out_shape=jax.ShapeDtypeStruct((b, s, w), BF16),
        compiler_params=pltpu.CompilerParams(dimension_semantics=("arbitrary", "arbitrary"),
                                             vmem_limit_bytes=VMEM_LIMIT),
        name="sb_attn",
    )(qtab, ktab, q, k, v, tri, bias)


def _gla_block(q_ref, k_ref, v_ref, cum_ref, r_ref, g_ref, o_ref, st_ref):
    c = GLA_CHUNK
    lane = lax.broadcasted_iota(jnp.int32, (1, LANES), 1)
    row = lax.broadcasted_iota(jnp.int32, (c, c), 0)
    col = lax.broadcasted_iota(jnp.int32, (c, c), 1)
    causal = col <= row
    for n in range(GLA_BLK // c):
        if n % GLA_GROUP == 0:
            work = []
        rows = slice(n * c, (n + 1) * c)
        for p in range(GLA_HEADS // 2):
            lanes = slice(p * LANES, (p + 1) * LANES)
            cum = cum_ref[rows, lanes]
            cum_last = cum[c - 1:c, :]
            qf = q_ref[rows, lanes].astype(F32)
            kf = k_ref[rows, lanes].astype(F32)
            q_dec = qf * jnp.exp(cum)
            k_inv = (kf * jnp.exp(-cum)).astype(BF16)
            k_end = kf * jnp.exp(cum_last - cum)
            decay = jnp.exp(cum_last)
            for hh in range(2):
                h = 2 * p + hh
                hmask = (lane >= hh * GLA_DK) & (lane < (hh + 1) * GLA_DK)
                vlanes = slice(h * GLA_DV, (h + 1) * GLA_DV)
                vb = v_ref[rows, vlanes]
                qd = jnp.where(hmask, q_dec, 0.0).astype(BF16)
                ke = jnp.where(hmask, k_end, 0.0).astype(BF16)
                att = _nt_dot(qd, k_inv)
                vt = vb.astype(F32).T.astype(BF16)
                d_state = _dot(vt, ke)
                work.append((rows, h, vlanes, vb, qd, att, d_state, decay))
        if (n + 1) % GLA_GROUP != 0:
            continue
        for rows, h, vlanes, vb, qd, att, d_state, decay in work:
            att = jnp.where(causal, att, 0.0)
            st = st_ref[h]
            o = _dot(att.astype(BF16), vb) + _nt_dot(qd, st.astype(BF16))
            st_ref[h] = st * decay + d_state
            o = o * lax.rsqrt(jnp.mean(o * o, axis=-1, keepdims=True) + EPS)
            o = o * g_ref[0:1, vlanes] * r_ref[rows, vlanes].astype(F32)
            o_ref[rows, vlanes] = o.astype(BF16)


def _gla_mix_kernel(x_ref, osb_ref, gq_ref, gk_ref, gv_ref, cum_ref, r_ref, g_ref, sgs_ref, sgg_ref,
                    wsb_ref, wgla_ref, wout_ref, h_ref, st_ref, ogla_ref, y_ref, *, tiles_per_seq):
    @pl.when(pl.program_id(0) % tiles_per_seq == 0)
    def _():
        st_ref[...] = jnp.zeros_like(st_ref)

    y_ref[...] = sgs_ref[...].astype(F32) * _dot(osb_ref[...], wsb_ref[...])
    _gla_block(gq_ref, gk_ref, gv_ref, cum_ref, r_ref, g_ref, ogla_ref, st_ref)
    halves = [slice(0, GLA_BLK // 2), slice(GLA_BLK // 2, GLA_BLK)]
    ys = [y_ref[r, :] + sgg_ref[r, :].astype(F32) * _dot(ogla_ref[r, :], wgla_ref[...])
          for r in halves]
    for r, y in zip(halves, ys):
        h_ref[r, :] = x_ref[r, :] + _dot(y.astype(BF16), wout_ref[...])


def _gla_mix(x2, seq_len, o_sb, gq, gk, gv, cum, r_act, norm_g, sg_sb, sg_gla,
             w_branch_sb, w_branch_gla, w_out):
    t, d = x2.shape
    vw = gv.shape[1]
    tok = lambda a: pl.BlockSpec((GLA_BLK, a.shape[1]), lambda i: (i, 0))
    wsb, wgla, wout = (w.astype(BF16) for w in (w_branch_sb, w_branch_gla, w_out))
    tokens = (x2, o_sb, gq, gk, gv, cum, r_act)
    gates = (sg_sb, sg_gla)
    consts = (wsb, wgla, wout)
    return pl.pallas_call(
        functools.partial(_gla_mix_kernel, tiles_per_seq=seq_len // GLA_BLK),
        grid=(t // GLA_BLK,),
        in_specs=([tok(a) for a in tokens] + [_const_spec((1, vw))] + [tok(a) for a in gates]
                  + [_const_spec(w.shape) for w in consts]),
        out_specs=tok(x2),
        out_shape=jax.ShapeDtypeStruct((t, d), F32),
        scratch_shapes=[pltpu.VMEM((GLA_HEADS, GLA_DV, LANES), F32),
                        pltpu.VMEM((GLA_BLK, vw), BF16),
                        pltpu.VMEM((GLA_BLK, d), F32)],
        compiler_params=pltpu.CompilerParams(dimension_semantics=("arbitrary",),
                                             vmem_limit_bytes=VMEM_LIMIT),
        name="gla_mix",
    )(*tokens, norm_g.reshape(1, vw), *gates, *consts)


GELU_C0 = 0.7978845608028654
GELU_K = 0.044715 ** 0.5 / GELU_C0


def _conv_ffn_kernel(h_ref, g_ref, win_ref, cw_ref, cb_ref, wout_ref, gf_ref, o_ref,
                     hn_ref, act_ref, acc_ref, prev_ref, *, tiles_per_seq, final_norm):
    n_ck = wout_ref.shape[0]
    d_ff = n_ck * FF_CK
    tm = FF_TM

    @pl.when(pl.program_id(0) % tiles_per_seq == 0)
    def _():
        prev_ref[...] = jnp.zeros_like(prev_ref)

    def tile(sub):
        rows = slice(sub * tm, (sub + 1) * tm)
        h = h_ref[rows, :]
        ms = jnp.mean(h * h, axis=-1, keepdims=True)
        hn_ref[sub] = (h * lax.rsqrt(ms + EPS) * g_ref[...]).astype(BF16)

        def out_proj(c):
            part = _dot(act_ref[sub, c % 2], wout_ref[c])
            if c == 0:
                acc_ref[sub] = part
            else:
                acc_ref[sub] += part

        for c in range(n_ck):
            hn = hn_ref[sub]
            a = _dot(hn, win_ref[:, c * FF_CK:(c + 1) * FF_CK])
            gate = _dot(hn, win_ref[:, d_ff + c * FF_CK:d_ff + (c + 1) * FF_CK])
            if c > 0:
                out_proj(c - 1)
            ext = jnp.concatenate([prev_ref[c], a], axis=0)
            prev_ref[c] = a[tm - SUBLANES:, :]
            a1 = pltpu.roll(ext, 1, 0)[SUBLANES:, :]
            a2 = pltpu.roll(ext, 2, 0)[SUBLANES:, :]
            cw = cw_ref[c]
            y = cb_ref[c] + a2 * cw[0:1, :] + a1 * cw[1:2, :] + a * cw[2:3, :]
            yb = y.astype(BF16)
            sb = (y * GELU_K).astype(BF16)
            gb = gate.astype(BF16)
            act_ref[sub, c % 2] = (yb * gb) * (1.0 + jnp.tanh(yb * (1.0 + sb * sb)))
        out_proj(n_ck - 1)
        h2 = h + acc_ref[sub]
        if final_norm:
            ms2 = jnp.mean(h2 * h2, axis=-1, keepdims=True)
            h2 = h2 * lax.rsqrt(ms2 + EPS) * gf_ref[...]
        o_ref[rows, :] = h2

    for sub in range(FF_SUB):
        tile(sub)


def _conv_ffn(h2d, seq_len, norm_g, w_ffn_in, conv_w, conv_b, w_ffn_out, norm_final_g, final_norm):
    t, d = h2d.shape
    d_ff = w_ffn_out.shape[0]
    n_ck = d_ff // FF_CK
    blk = FF_SUB * FF_TM
    tok = pl.BlockSpec((blk, d), lambda i: (i, 0))
    chunked = lambda w: w.reshape(w.shape[0], n_ck, FF_CK).transpose(1, 0, 2)
    win = w_ffn_in.astype(BF16)
    wout = ((0.5 / GELU_C0) * w_ffn_out).astype(BF16).reshape(n_ck, FF_CK, d)
    cw = chunked(GELU_C0 * conv_w)
    cb = chunked(GELU_C0 * conv_b.reshape(1, d_ff))
    kern = functools.partial(_conv_ffn_kernel, tiles_per_seq=seq_len // blk, final_norm=final_norm)
    return pl.pallas_call(
        kern,
        grid=(t // blk,),
        in_specs=[tok, _const_spec((1, d)), _const_spec(win.shape), _const_spec(cw.shape),
                  _const_spec(cb.shape), _const_spec(wout.shape), _const_spec((1, d))],
        out_specs=tok,
        out_shape=jax.ShapeDtypeStruct((t, d), F32),
        scratch_shapes=[pltpu.VMEM((FF_SUB, FF_TM, d), BF16),
                        pltpu.VMEM((FF_SUB, 2, FF_TM, FF_CK), BF16),
                        pltpu.VMEM((FF_SUB, FF_TM, d), F32),
                        pltpu.VMEM((n_ck, SUBLANES, FF_CK), F32)],
        compiler_params=pltpu.CompilerParams(dimension_semantics=("arbitrary",),
                                             vmem_limit_bytes=VMEM_LIMIT),
        name="conv_ffn",
    )(h2d, norm_g.reshape(1, d), win, cw, cb, wout, norm_final_g.reshape(1, d))


def kernel(x, norm_mix_g, w_in, b_gate, w_alpha_up, b_alpha, gla_norm_g, w_branch_sb, w_branch_gla,
           w_out, norm_ffn_g, w_ffn_in, conv_w, conv_b, w_ffn_out, norm_final_g):
    b, s, d = x.shape
    depth = w_in.shape[0]
    assert s % (IN_SUB * IN_TM) == 0 and IN_TM % GLA_CHUNK == 0
    assert s % SB_TQ == 0 and s % GLA_BLK == 0
    assert w_ffn_out.shape[1] % FF_CK == 0 and conv_w.shape[1] == CONV_WIDTH
    assert SB_UNROLL % 4 == 0 and s % (FF_SUB * FF_TM) == 0
    h = x.reshape(b * s, d)
    for i in range(depth):
        (sbq, sbk, sbv, gq, gk, gv, r_act, cum, sg_sb, sg_gla) = _in_proj(
            h, norm_mix_g[i], w_in[i], b_gate[i], w_alpha_up[i], b_alpha[i])
        seq = lambda a: a.reshape(b, s, a.shape[-1])
        o_sb = _sb_attn(seq(sbq), seq(sbk), seq(sbv)).reshape(b * s, -1)
        h = _gla_mix(h, s, o_sb, gq, gk, gv, cum, r_act, gla_norm_g[i], sg_sb, sg_gla,
                     w_branch_sb[i], w_branch_gla[i], w_out[i])
        h = _conv_ffn(h, s, norm_ffn_g[i], w_ffn_in[i], conv_w[i], conv_b[i], w_ffn_out[i],
                      norm_final_g, final_norm=(i == depth - 1))
    return h.reshape(b, s, d)
```

```python
import functools

import jax
import jax.numpy as jnp
from jax import lax
from jax.experimental import pallas as pl
from jax.experimental.pallas import tpu as pltpu

F32 = jnp.float32
BF16 = jnp.bfloat16

EPS = 1e-6
LOG2_E = 1.4426950408889634
SB_HEADS = 8
SB_DH = 64
GLA_HEADS = 4
GLA_DK = 64
GLA_DV = 128
GLA_RANK = 16
GLA_TAU = 16.0
GLA_CHUNK = 64
CONV_WIDTH = 3

LANES = 128
SUBLANES = 8
VMEM_LIMIT = 56 * 1024 * 1024

IN_TM = 256
IN_SUB = 2
FF_TM = 256
FF_SUB = 2
SB_TQ = 256
SB_TK = 256
GLA_BLK = 512
GLA_GROUP = 2
FF_CK = 256


def _nt_dot(a, b):
    return lax.dot_general(a, b, (((1,), (1,)), ((), ())), preferred_element_type=F32)


def _dot(a, b):
    return jnp.dot(a, b, preferred_element_type=F32)


def _log_sigmoid(z):
    return jnp.minimum(z, 0.0) - jnp.log(1.0 + jnp.exp(-jnp.abs(z)))


def _sigmoid(z):
    return 1.0 / (1.0 + jnp.exp(-z))


def _const_spec(shape):
    return pl.BlockSpec(shape, lambda *_: (0,) * len(shape))


def _in_proj_kernel(x_ref, g_ref, wqkv_ref, wa_ref, wup_ref, balpha_ref, wgate_ref, bgate_ref,
                    tri_ref, sbq_ref, sbk_ref, sbv_ref, gq_ref, gk_ref, gv_ref, gr_ref, cum_ref,
                    sgs_ref, sgg_ref):
    sb_w = SB_HEADS * SB_DH
    kw = GLA_HEADS * GLA_DK
    vw = GLA_HEADS * GLA_DV
    d = sgs_ref.shape[-1]

    def tile(rows):
        x = x_ref[rows, :]
        ms = jnp.mean(x * x, axis=-1, keepdims=True)
        xn = (x * lax.rsqrt(ms + EPS) * g_ref[...]).astype(BF16)

        def proj(lo, hi):
            return _dot(xn, wqkv_ref[:, lo:hi])

        def per_head(out_ref, val):
            lane = lax.broadcasted_iota(jnp.int32, (1, LANES), 1)
            for h in range(SB_HEADS):
                pair = val[:, (h // 2) * LANES:(h // 2 + 1) * LANES]
                hmask = (lane >= (h % 2) * SB_DH) & (lane < (h % 2 + 1) * SB_DH)
                out_ref[rows, h * LANES:(h + 1) * LANES] = jnp.where(hmask, pair, 0.0).astype(BF16)

        code = _dot(xn, wa_ref[...])
        o = 0
        per_head(sbq_ref, proj(o, o + sb_w) * (SB_DH ** -0.5 * LOG2_E)); o += sb_w
        a_pre = _dot(code.astype(BF16), wup_ref[...]) + balpha_ref[...]
        sbk_ref[rows, :] = proj(o, o + sb_w).astype(BF16); o += sb_w
        per_head(sbv_ref, proj(o, o + sb_w)); o += sb_w
        log_a = _log_sigmoid(a_pre) * (1.0 / GLA_TAU)
        la_hi = log_a.astype(BF16)
        la_lo = (log_a - la_hi.astype(F32)).astype(BF16)
        tri = tri_ref[...]
        cum_ref[rows, :] = _dot(tri, la_hi) + _dot(tri, la_lo)
        gq_ref[rows, :] = (proj(o, o + kw) * (GLA_DK ** -0.5)).astype(BF16); o += kw
        gk_ref[rows, :] = proj(o, o + kw).astype(BF16); o += kw
        gv_ref[rows, :] = proj(o, o + vw).astype(BF16); o += vw
        r = proj(o, o + vw)
        gr_ref[rows, :] = (r * _sigmoid(r)).astype(BF16)
        sgs_ref[rows, :] = _sigmoid(_dot(xn, wgate_ref[:, :d]) + bgate_ref[0:1, :]).astype(BF16)
        sgg_ref[rows, :] = _sigmoid(_dot(xn, wgate_ref[:, d:]) + bgate_ref[1:2, :]).astype(BF16)

    for sub in range(IN_SUB):
        tile(slice(sub * IN_TM, (sub + 1) * IN_TM))


def _in_proj(x2, norm_g, w_in, b_gate, w_alpha_up, b_alpha):
    t, d = x2.shape
    sb_w = SB_HEADS * SB_DH
    kw = GLA_HEADS * GLA_DK
    vw = GLA_HEADS * GLA_DV
    n_qkv = 3 * sb_w + 2 * kw + 2 * vw
    wqkv = w_in[:, :n_qkv].astype(BF16)
    wa = jnp.pad(w_in[:, n_qkv:n_qkv + GLA_RANK], ((0, 0), (0, LANES - GLA_RANK))).astype(BF16)
    wgate = w_in[:, n_qkv + GLA_RANK:].astype(BF16)
    wup = jnp.pad(w_alpha_up, ((0, LANES - GLA_RANK), (0, 0))).astype(BF16)
    row = lax.broadcasted_iota(jnp.int32, (IN_TM, IN_TM), 0)
    col = lax.broadcasted_iota(jnp.int32, (IN_TM, IN_TM), 1)
    tri = ((row // GLA_CHUNK == col // GLA_CHUNK) & (col <= row)).astype(BF16)

    blk = IN_SUB * IN_TM
    tok = lambda w: pl.BlockSpec((blk, w), lambda i: (i, 0))
    head_w = SB_HEADS * LANES
    out_shapes = [jax.ShapeDtypeStruct((t, w), BF16)
                  for w in (head_w, sb_w, head_w, kw, kw, vw, vw)]
    out_shapes += [jax.ShapeDtypeStruct((t, kw), F32),
                   jax.ShapeDtypeStruct((t, d), BF16), jax.ShapeDtypeStruct((t, d), BF16)]
    out_specs = [tok(s.shape[1]) for s in out_shapes]
    return pl.pallas_call(
        _in_proj_kernel,
        grid=(t // blk,),
        in_specs=[tok(d), _const_spec((1, d)), _const_spec(wqkv.shape), _const_spec(wa.shape),
                  _const_spec(wup.shape), _const_spec((1, kw)), _const_spec(wgate.shape),
                  _const_spec((2, d)), _const_spec((IN_TM, IN_TM))],
        out_specs=out_specs,
        out_shape=out_shapes,
        compiler_params=pltpu.CompilerParams(dimension_semantics=("arbitrary",),
                                             vmem_limit_bytes=VMEM_LIMIT),
        name="in_proj",
    )(x2, norm_g.reshape(1, d), wqkv, wa, wup, b_alpha.reshape(1, kw), wgate, b_gate, tri)


SB_STAGES = 4
SB_UNROLL = 12


def _sb_kernel(qtab_ref, ktab_ref, q0_ref, q1_ref, k_ref, v0_ref, v1_ref, tri_ref, bias_ref, o_ref,
               z_buf, sp_buf, w_buf, carry_ref, acc_ref, *, n_tiles):
    q_refs = (q0_ref, q1_ref)
    v_refs = (v0_ref, v1_ref)
    carry_ref[...] = jnp.zeros_like(carry_ref)
    acc_ref[...] = jnp.zeros_like(acc_ref)

    def rows(tile_idx, size):
        return pl.ds(pl.multiple_of(tile_idx * size, size), size)

    def has_tile(i, stage):
        return not isinstance(i, int) or 0 <= i - (stage - 1) < n_tiles

    def step(i, u):
        if has_tile(i, 3):
            a3 = qtab_ref[i - 2]
            p3 = u & 1
            z3 = (u + 2) & 3
            neg_tri = tri_ref[...]
            for hh in range(2):
                x = _dot(sp_buf[p3, hh], neg_tri)
                carry = carry_ref[a3, hh]
                lw = z_buf[z3, hh] + x + jnp.concatenate([carry] * (SB_TK // LANES), axis=1)
                w_buf[p3, hh] = jnp.exp2(lw).astype(BF16)
                carry_ref[a3, hh] = carry + jnp.broadcast_to(x[:, 0:1], (SB_TQ, LANES))

        if has_tile(i, 4):
            q4 = qtab_ref[i - 3]
            k4 = ktab_ref[i - 3]
            p4 = (u + 1) & 1
            pv = (_dot(w_buf[p4, 0], v_refs[0][0, rows(k4, SB_TK), :])
                  + _dot(w_buf[p4, 1], v_refs[1][0, rows(k4, SB_TK), :]))
            acc = acc_ref[q4] + pv
            acc_ref[q4] = acc
            o_ref[0, rows(q4, SB_TQ), :] = acc.astype(BF16)

        if has_tile(i, 2):
            p2 = (u + 1) & 1
            z2 = (u + 3) & 3
            for hh in range(2):
                z = z_buf[z2, hh]
                sp = jnp.maximum(z, 0.0) + jnp.log(1.0 + jnp.exp2(-jnp.abs(z))) * LOG2_E
                sp_buf[p2, hh] = sp.astype(BF16)

        if has_tile(i, 1):
            q1 = qtab_ref[i]
            k1 = ktab_ref[i]
            z1 = u & 3
            kt = k_ref[0, rows(k1, SB_TK), :]
            bias = bias_ref[(k1 == q1).astype(jnp.int32)]
            for hh in range(2):
                z_buf[z1, hh] = _nt_dot(q_refs[hh][0, rows(q1, SB_TQ), :], kt) + bias

    def body(trip, _):
        for u in range(SB_UNROLL):
            step(trip * SB_UNROLL + u, u)
        return 0

    for i in range(SB_UNROLL):
        step(i, i)
    lax.fori_loop(1, n_tiles // SB_UNROLL, body, 0)
    for i in range(n_tiles, n_tiles + SB_STAGES - 1):
        step(i, i % SB_UNROLL)


def _sb_attn(q, k, v):
    b, s, w = k.shape
    nq = s // SB_TQ
    tiles = [(qi, kj) for qi in range(nq) for kj in range(qi, -1, -1)]
    n_tiles = len(tiles)
    assert n_tiles % SB_UNROLL == 0 and SB_UNROLL >= SB_STAGES
    qtab, ktab = (jnp.asarray([t[c] for t in tiles], jnp.int32) for c in range(2))
    row = lax.broadcasted_iota(jnp.int32, (SB_TK, SB_TK), 0)
    col = lax.broadcasted_iota(jnp.int32, (SB_TK, SB_TK), 1)
    tri = -(row >= col).astype(BF16)
    bias = jnp.stack([jnp.zeros((SB_TQ, SB_TK), F32),
                      jnp.where(col < row, 0.0, -1e30).astype(F32)])
    seq_spec = pl.BlockSpec((1, s, LANES), lambda bi, p, *_: (bi, 0, p))
    head_spec = lambda hh: pl.BlockSpec((1, s, LANES), lambda bi, p, *_: (bi, 0, 2 * p + hh))
    grid_spec = pltpu.PrefetchScalarGridSpec(
        num_scalar_prefetch=2,
        grid=(b, w // LANES),
        in_specs=[head_spec(0), head_spec(1), seq_spec, head_spec(0), head_spec(1),
                  pl.BlockSpec((SB_TK, SB_TK), lambda bi, p, *_: (0, 0)),
                  pl.BlockSpec((2, SB_TQ, SB_TK), lambda bi, p, *_: (0, 0, 0))],
        out_specs=seq_spec,
        scratch_shapes=[pltpu.VMEM((4, 2, SB_TQ, SB_TK), F32),
                        pltpu.VMEM((2, 2, SB_TQ, SB_TK), BF16),
                        pltpu.VMEM((2, 2, SB_TQ, SB_TK), BF16),
                        pltpu.VMEM((nq, 2, SB_TQ, LANES), F32),
                        pltpu.VMEM((nq, SB_TQ, LANES), F32)],
    )
    return pl.pallas_call(
        functools.partial(_sb_kernel, n_tiles=n_tiles),
        grid_spec=grid_spec,
        out_shape=jax.ShapeDtypeStruct((b, s, w), BF16),
        compiler_params=pltpu.CompilerParams(dimension_semantics=("arbitrary", "arbitrary"),
                                             vmem_limit_bytes=VMEM_LIMIT),
        name="sb_attn",
    )(qtab, ktab, q, q, k, v, v, tri, bias)


def _gla_block(q_ref, k_ref, v_ref, cum_ref, r_ref, g_ref, o_ref, st_ref):
    c = GLA_CHUNK
    lane = lax.broadcasted_iota(jnp.int32, (1, LANES), 1)
    row = lax.broadcasted_iota(jnp.int32, (c, c), 0)
    col = lax.broadcasted_iota(jnp.int32, (c, c), 1)
    causal = col <= row
    for n in range(GLA_BLK // c):
        if n % GLA_GROUP == 0:
            work = []
        rows = slice(n * c, (n + 1) * c)
        for p in range(GLA_HEADS // 2):
            lanes = slice(p * LANES, (p + 1) * LANES)
            cum = cum_ref[rows, lanes]
            cum_last = cum[c - 1:c, :]
            qf = q_ref[rows, lanes].astype(F32)
            kf = k_ref[rows, lanes].astype(F32)
            q_dec = qf * jnp.exp(cum)
            k_inv = (kf * jnp.exp(-cum)).astype(BF16)
            k_end = kf * jnp.exp(cum_last - cum)
            decay = jnp.exp(cum_last)
            for hh in range(2):
                h = 2 * p + hh
                hmask = (lane >= hh * GLA_DK) & (lane < (hh + 1) * GLA_DK)
                vlanes = slice(h * GLA_DV, (h + 1) * GLA_DV)
                vb = v_ref[rows, vlanes]
                qd = jnp.where(hmask, q_dec, 0.0).astype(BF16)
                ke = jnp.where(hmask, k_end, 0.0).astype(BF16)
                att = _nt_dot(qd, k_inv)
                vt = vb.astype(F32).T.astype(BF16)
                d_state = _dot(vt, ke)
                work.append((rows, h, vlanes, vb, qd, att, d_state, decay))
        if (n + 1) % GLA_GROUP != 0:
            continue
        for rows, h, vlanes, vb, qd, att, d_state, decay in work:
            att = jnp.where(causal, att, 0.0)
            st = st_ref[h]
            o = _dot(att.astype(BF16), vb) + _nt_dot(qd, st.astype(BF16))
            st_ref[h] = st * decay + d_state
            o = o * lax.rsqrt(jnp.mean(o * o, axis=-1, keepdims=True) + EPS)
            o = o * g_ref[0:1, vlanes] * r_ref[rows, vlanes].astype(F32)
            o_ref[rows, vlanes] = o.astype(BF16)


def _gla_mix_kernel(x_ref, osb_ref, gq_ref, gk_ref, gv_ref, cum_ref, r_ref, g_ref, sgs_ref, sgg_ref,
                    wsb_ref, wgla_ref, wout_ref, h_ref, st_ref, ogla_ref, y_ref, *, tiles_per_seq):
    @pl.when(pl.program_id(0) % tiles_per_seq == 0)
    def _():
        st_ref[...] = jnp.zeros_like(st_ref)

    y_ref[...] = sgs_ref[...].astype(F32) * _dot(osb_ref[...], wsb_ref[...])
    _gla_block(gq_ref, gk_ref, gv_ref, cum_ref, r_ref, g_ref, ogla_ref, st_ref)
    halves = [slice(0, GLA_BLK // 2), slice(GLA_BLK // 2, GLA_BLK)]
    ys = [y_ref[r, :] + sgg_ref[r, :].astype(F32) * _dot(ogla_ref[r, :], wgla_ref[...])
          for r in halves]
    for r, y in zip(halves, ys):
        h_ref[r, :] = x_ref[r, :] + _dot(y.astype(BF16), wout_ref[...])


def _gla_mix(x2, seq_len, o_sb, gq, gk, gv, cum, r_act, norm_g, sg_sb, sg_gla,
             w_branch_sb, w_branch_gla, w_out):
    t, d = x2.shape
    vw = gv.shape[1]
    tok = lambda a: pl.BlockSpec((GLA_BLK, a.shape[1]), lambda i: (i, 0))
    wsb, wgla, wout = (w.astype(BF16) for w in (w_branch_sb, w_branch_gla, w_out))
    tokens = (x2, o_sb, gq, gk, gv, cum, r_act)
    gates = (sg_sb, sg_gla)
    consts = (wsb, wgla, wout)
    return pl.pallas_call(
        functools.partial(_gla_mix_kernel, tiles_per_seq=seq_len // GLA_BLK),
        grid=(t // GLA_BLK,),
        in_specs=([tok(a) for a in tokens] + [_const_spec((1, vw))] + [tok(a) for a in gates]
                  + [_const_spec(w.shape) for w in consts]),
        out_specs=tok(x2),
        out_shape=jax.ShapeDtypeStruct((t, d), F32),
        scratch_shapes=[pltpu.VMEM((GLA_HEADS, GLA_DV, LANES), F32),
                        pltpu.VMEM((GLA_BLK, vw), BF16),
                        pltpu.VMEM((GLA_BLK, d), F32)],
        compiler_params=pltpu.CompilerParams(dimension_semantics=("arbitrary",),
                                             vmem_limit_bytes=VMEM_LIMIT),
        name="gla_mix",
    )(*tokens, norm_g.reshape(1, vw), *gates, *consts)


GELU_C0 = 0.7978845608028654
GELU_K = 0.044715 ** 0.5 / GELU_C0


def _conv_ffn_kernel(h_ref, g_ref, win_ref, cw_ref, cb_ref, wout_ref, gf_ref, o_ref,
                     hn_ref, act_ref, acc_ref, prev_ref, *, tiles_per_seq, final_norm):
    n_ck = wout_ref.shape[0]
    d_ff = n_ck * FF_CK
    tm = FF_TM

    @pl.when(pl.program_id(0) % tiles_per_seq == 0)
    def _():
        prev_ref[...] = jnp.zeros_like(prev_ref)

    def tile(sub):
        rows = slice(sub * tm, (sub + 1) * tm)
        h = h_ref[rows, :]
        ms = jnp.mean(h * h, axis=-1, keepdims=True)
        hn_ref[sub] = (h * lax.rsqrt(ms + EPS) * g_ref[...]).astype(BF16)

        def out_proj(c):
            part = _dot(act_ref[sub, c % 2], wout_ref[c])
            if c == 0:
                acc_ref[sub] = part
            else:
                acc_ref[sub] += part

        for c in range(n_ck):
            hn = hn_ref[sub]
            a = _dot(hn, win_ref[:, c * FF_CK:(c + 1) * FF_CK])
            gate = _dot(hn, win_ref[:, d_ff + c * FF_CK:d_ff + (c + 1) * FF_CK])
            if c > 0:
                out_proj(c - 1)
            ext = jnp.concatenate([prev_ref[c], a], axis=0)
            prev_ref[c] = a[tm - SUBLANES:, :]
            a1 = pltpu.roll(ext, 1, 0)[SUBLANES:, :]
            a2 = pltpu.roll(ext, 2, 0)[SUBLANES:, :]
            cw = cw_ref[c]
            y = cb_ref[c] + a2 * cw[0:1, :] + a1 * cw[1:2, :] + a * cw[2:3, :]
            yb = y.astype(BF16)
            sb = (y * GELU_K).astype(BF16)
            gb = gate.astype(BF16)
            act_ref[sub, c % 2] = (yb * gb) * (1.0 + jnp.tanh(yb * (1.0 + sb * sb)))
        out_proj(n_ck - 1)
        h2 = h + acc_ref[sub]
        if final_norm:
            ms2 = jnp.mean(h2 * h2, axis=-1, keepdims=True)
            h2 = h2 * lax.rsqrt(ms2 + EPS) * gf_ref[...]
        o_ref[rows, :] = h2

    for sub in range(FF_SUB):
        tile(sub)


def _conv_ffn(h2d, seq_len, norm_g, w_ffn_in, conv_w, conv_b, w_ffn_out, norm_final_g, final_norm):
    t, d = h2d.shape
    d_ff = w_ffn_out.shape[0]
    n_ck = d_ff // FF_CK
    blk = FF_SUB * FF_TM
    tok = pl.BlockSpec((blk, d), lambda i: (i, 0))
    chunked = lambda w: w.reshape(w.shape[0], n_ck, FF_CK).transpose(1, 0, 2)
    win = w_ffn_in.astype(BF16)
    wout = ((0.5 / GELU_C0) * w_ffn_out).astype(BF16).reshape(n_ck, FF_CK, d)
    cw = chunked(GELU_C0 * conv_w)
    cb = chunked(GELU_C0 * conv_b.reshape(1, d_ff))
    kern = functools.partial(_conv_ffn_kernel, tiles_per_seq=seq_len // blk, final_norm=final_norm)
    return pl.pallas_call(
        kern,
        grid=(t // blk,),
        in_specs=[tok, _const_spec((1, d)), _const_spec(win.shape), _const_spec(cw.shape),
                  _const_spec(cb.shape), _const_spec(wout.shape), _const_spec((1, d))],
        out_specs=tok,
        out_shape=jax.ShapeDtypeStruct((t, d), F32),
        scratch_shapes=[pltpu.VMEM((FF_SUB, FF_TM, d), BF16),
                        pltpu.VMEM((FF_SUB, 2, FF_TM, FF_CK), BF16),
                        pltpu.VMEM((FF_SUB, FF_TM, d), F32),
                        pltpu.VMEM((n_ck, SUBLANES, FF_CK), F32)],
        compiler_params=pltpu.CompilerParams(dimension_semantics=("arbitrary",),
                                             vmem_limit_bytes=VMEM_LIMIT),
        name="conv_ffn",
    )(h2d, norm_g.reshape(1, d), win, cw, cb, wout, norm_final_g.reshape(1, d))


def kernel(x, norm_mix_g, w_in, b_gate, w_alpha_up, b_alpha, gla_norm_g, w_branch_sb, w_branch_gla,
           w_out, norm_ffn_g, w_ffn_in, conv_w, conv_b, w_ffn_out, norm_final_g):
    b, s, d = x.shape
    depth = w_in.shape[0]
    assert s % (IN_SUB * IN_TM) == 0 and IN_TM % GLA_CHUNK == 0
    assert s % SB_TQ == 0 and s % GLA_BLK == 0
    assert w_ffn_out.shape[1] % FF_CK == 0 and conv_w.shape[1] == CONV_WIDTH
    assert SB_UNROLL % 4 == 0 and s % (FF_SUB * FF_TM) == 0
    h = x.reshape(b * s, d)
    for i in range(depth):
        (sbq, sbk, sbv, gq, gk, gv, r_act, cum, sg_sb, sg_gla) = _in_proj(
            h, norm_mix_g[i], w_in[i], b_gate[i], w_alpha_up[i], b_alpha[i])
        seq = lambda a: a.reshape(b, s, a.shape[-1])
        o_sb = _sb_attn(seq(sbq), seq(sbk), seq(sbv)).reshape(b * s, -1)
        h = _gla_mix(h, s, o_sb, gq, gk, gv, cum, r_act, gla_norm_g[i], sg_sb, sg_gla,
                     w_branch_sb[i], w_branch_gla[i], w_out[i])
        h = _conv_ffn(h, s, norm_ffn_g[i], w_ffn_in[i], conv_w[i], conv_b[i], w_ffn_out[i],
                      norm_final_g, final_norm=(i == depth - 1))
    return h.reshape(b, s, d)
```
